```python
import jax, jax.numpy as jnp
from jax import lax
import numpy as np

D_MODEL = 1024
BATCH = 16
SEQ = 2048
DEPTH = 1

P_DIM = 256
LRU_WIDTH = 1024
LRU_HEADS = 8
LRU_HEAD_DIM = LRU_WIDTH // LRU_HEADS
CONV_WIDTH = 4
LRU_C = 8.0
POOL_WIDTH = D_MODEL // 2
POOL_WINDOWS = (2, 4, 8, 16)
POOL_GROUPS = len(POOL_WINDOWS)
POOL_GROUP_DIM = POOL_WIDTH // POOL_GROUPS
MAX_WIN = max(POOL_WINDOWS)
IN_COLS = 2 * LRU_WIDTH + 2 * POOL_WIDTH + 2 * D_MODEL
EPS = 1e-6

kernel_name = "hybrid_rglru_pool_gated_merge"


def rmsnorm(x, g):
    x32 = x.astype(jnp.float32)
    ms = jnp.mean(x32 * x32, axis=-1, keepdims=True)
    return (x32 * lax.rsqrt(ms + EPS)).astype(x.dtype) * g


def causal_depthwise_conv(x, w, b):
    s = x.shape[1]
    xp = jnp.pad(x, ((0, 0), (CONV_WIDTH - 1, 0), (0, 0)))
    y = b
    for k in range(CONV_WIDTH):
        y = y + xp[:, k:k + s, :] * w[k]
    return y


def block_diag_linear(x, w, b):
    bsz, s, _ = x.shape
    h, dh, _ = w.shape
    xh = x.reshape(bsz, s, h, dh)
    y = jnp.einsum('bshd,hde->bshe', xh, w) + b
    return y.reshape(bsz, s, h * dh)


def rg_lru(x, w_a, b_a, w_x, b_x, lam):
    x32 = x.astype(jnp.float32)
    r = jax.nn.sigmoid(block_diag_linear(x, w_a, b_a).astype(jnp.float32))
    i = jax.nn.sigmoid(block_diag_linear(x, w_x, b_x).astype(jnp.float32))
    log_a = -LRU_C * r * jax.nn.softplus(-lam.astype(jnp.float32))
    a = jnp.exp(log_a)
    mult = jnp.sqrt(-jnp.expm1(2.0 * log_a))
    u = mult * (i * x32)

    def combine(c1, c2):
        a1, b1 = c1
        a2, b2 = c2
        return a2 * a1, a2 * b1 + b2

    _, h = lax.associative_scan(combine, (a, u), axis=1)
    return h.astype(x.dtype)


def multiscale_pool(x, w_pool, scale):
    bsz, s, _ = x.shape
    x32 = x.astype(jnp.float32)
    c = jnp.cumsum(x32, axis=1)
    c_pad = jnp.pad(c, ((0, 0), (MAX_WIN, 0), (0, 0)))
    pos = jnp.arange(s)
    outs = []
    for g, k in enumerate(POOL_WINDOWS):
        cg = c_pad[..., g * POOL_GROUP_DIM:(g + 1) * POOL_GROUP_DIM]
        win_sum = cg[:, MAX_WIN:, :] - cg[:, MAX_WIN - k:MAX_WIN - k + s, :]
        count = jnp.minimum(pos + 1, k).astype(jnp.float32)[None, :, None]
        outs.append(win_sum / count)
    pooled = jnp.concatenate(outs, axis=-1)
    diff = (pooled - x32).astype(x.dtype).reshape(bsz, s, POOL_GROUPS, POOL_GROUP_DIM)
    y = jnp.einsum('bsgd,gde->bsge', diff, w_pool).reshape(bsz, s, POOL_WIDTH)
    return y * scale


def setup_inputs(seed: int = 0) -> dict:
    key = jax.random.key(seed)
    ks = jax.random.split(key, 24)
    f32 = jnp.float32

    def nrm(k, shape, fan_in):
        return jax.random.normal(k, shape, f32) * (fan_in ** -0.5)

    x = jax.random.normal(ks[0], (BATCH, SEQ, D_MODEL), f32)
    p = jax.random.normal(ks[1], (DEPTH, BATCH, SEQ, P_DIM), f32)
    norm_g = 1.0 + 0.05 * jax.random.normal(ks[2], (DEPTH, D_MODEL), f32)
    w_in = nrm(ks[3], (DEPTH, D_MODEL, IN_COLS), D_MODEL)
    conv_w = nrm(ks[4], (DEPTH, CONV_WIDTH, LRU_WIDTH), CONV_WIDTH)
    conv_b = 0.02 * jax.random.normal(ks[5], (DEPTH, LRU_WIDTH), f32)
    lru_w_a = nrm(ks[6], (DEPTH, LRU_HEADS, LRU_HEAD_DIM, LRU_HEAD_DIM), LRU_HEAD_DIM)
    lru_b_a = 0.02 * jax.random.normal(ks[7], (DEPTH, LRU_HEADS, LRU_HEAD_DIM), f32)
    lru_w_x = nrm(ks[8], (DEPTH, LRU_HEADS, LRU_HEAD_DIM, LRU_HEAD_DIM), LRU_HEAD_DIM)
    lru_b_x = 0.02 * jax.random.normal(ks[9], (DEPTH, LRU_HEADS, LRU_HEAD_DIM), f32)
    u = jax.random.uniform(ks[10], (DEPTH, LRU_WIDTH), f32, 0.9, 0.999)
    sa = u ** (1.0 / LRU_C)
    lru_lambda = jnp.log(sa) - jnp.log1p(-sa)
    pool_w = nrm(ks[11], (DEPTH, POOL_GROUPS, POOL_GROUP_DIM, POOL_GROUP_DIM), POOL_GROUP_DIM)
    pool_scale = 1.0 + 0.1 * jax.random.normal(ks[12], (DEPTH, POOL_WIDTH), f32)
    w_proj_lru = nrm(ks[13], (DEPTH, LRU_WIDTH, D_MODEL), LRU_WIDTH)
    w_proj_pool = nrm(ks[14], (DEPTH, POOL_WIDTH, D_MODEL), POOL_WIDTH)
    w_out = nrm(ks[15], (DEPTH, D_MODEL, D_MODEL), D_MODEL)
    ple_norm_g = 1.0 + 0.05 * jax.random.normal(ks[16], (DEPTH, D_MODEL), f32)
    w_ple_gate = nrm(ks[17], (DEPTH, D_MODEL, D_MODEL), D_MODEL)
    w_ple_proj = nrm(ks[18], (DEPTH, P_DIM, D_MODEL), P_DIM)
    final_g = 1.0 + 0.05 * jax.random.normal(ks[19], (D_MODEL,), f32)
    return {
        "x": x, "p": p, "norm_g": norm_g, "w_in": w_in,
        "conv_w": conv_w, "conv_b": conv_b,
        "lru_w_a": lru_w_a, "lru_b_a": lru_b_a, "lru_w_x": lru_w_x, "lru_b_x": lru_b_x,
        "lru_lambda": lru_lambda, "pool_w": pool_w, "pool_scale": pool_scale,
        "w_proj_lru": w_proj_lru, "w_proj_pool": w_proj_pool, "w_out": w_out,
        "ple_norm_g": ple_norm_g, "w_ple_gate": w_ple_gate, "w_ple_proj": w_ple_proj,
        "final_g": final_g,
    }


def reference(x, p, norm_g, w_in, conv_w, conv_b, lru_w_a, lru_b_a, lru_w_x, lru_b_x,
              lru_lambda, pool_w, pool_scale, w_proj_lru, w_proj_pool, w_out,
              ple_norm_g, w_ple_gate, w_ple_proj, final_g):
    split_points = np.cumsum([LRU_WIDTH, LRU_WIDTH, POOL_WIDTH, POOL_WIDTH, D_MODEL]).tolist()
    for i in range(DEPTH):
        h = rmsnorm(x, norm_g[i])
        z = h @ w_in[i]
        xa, ga, xb, gb, ma, mb = jnp.split(z, split_points, axis=-1)
        xa = causal_depthwise_conv(xa, conv_w[i], conv_b[i])
        ya = rg_lru(xa, lru_w_a[i], lru_b_a[i], lru_w_x[i], lru_b_x[i], lru_lambda[i]) * jax.nn.silu(ga)
        yb = multiscale_pool(xb, pool_w[i], pool_scale[i]) * jax.nn.silu(gb)
        merged = jax.nn.sigmoid(ma) * (ya @ w_proj_lru[i]) + jax.nn.sigmoid(mb) * (yb @ w_proj_pool[i])
        x = x + merged @ w_out[i]
        gate = jax.nn.sigmoid(rmsnorm(x, ple_norm_g[i]) @ w_ple_gate[i])
        x = x + gate * (p[i] @ w_ple_proj[i])
    return rmsnorm(x, final_g)
```

```python
import functools

import jax
import jax.numpy as jnp
from jax import lax
from jax.experimental import pallas as pl
from jax.experimental.pallas import tpu as pltpu

D_MODEL = 1024
BATCH = 16
SEQ = 2048
P_DIM = 256
LRU_WIDTH = 1024
LRU_HEADS = 8
LRU_HEAD_DIM = 128
CONV_WIDTH = 4
LRU_C = 8.0
POOL_WIDTH = 512
POOL_WINDOWS = (2, 4, 8, 16)
POOL_GROUP_DIM = 128
MAX_WIN = 16
EPS = 1e-6

OFF_XA = 0
OFF_GA = LRU_WIDTH
OFF_XB = 2 * LRU_WIDTH
OFF_GB = OFF_XB + POOL_WIDTH
OFF_MA = OFF_GB + POOL_WIDTH
OFF_MB = OFF_MA + D_MODEL
IN_COLS = OFF_MB + D_MODEL

TT = 32
ROWS = TT * BATCH
CW = 256
N_CHUNKS = D_MODEL // CW
RT = 64
CONV_TAIL = (CONV_WIDTH - 1) * BATCH
POOL_TAIL = MAX_WIN * BATCH
VMEM_LIMIT_BYTES = 56 * 1024 * 1024

F32 = jnp.float32
BF16 = jnp.bfloat16


def _sigmoid(v):
    return 0.5 * jnp.tanh(0.5 * v) + 0.5


def _dot(a, b):
    return jnp.dot(a, b, preferred_element_type=F32)


def _block_kernel(x_ref, p_ref, ng_ref, win_ref, cw_ref, cb_ref, wax_ref, ba_ref, bx_ref,
                  lam_ref, pw_ref, ps_ref, plru_ref, ppool_ref, wout_ref, pg_ref, wpg_ref,
                  wpe_ref, fg_ref,
                  o_ref,
                  hb, pbf, xa_ext, xb_ext, hst, buf_a, buf_b, buf_c, buf_d, xcb, ya, yb,
                  dbuf, gbuf, mbuf, x1):
    ti = pl.program_id(0)

    @pl.when(ti == 0)
    def _():
        xa_ext[:, 0:CONV_TAIL, :] = jnp.zeros((N_CHUNKS, CONV_TAIL, CW), F32)
        xb_ext[0:POOL_TAIL, :] = jnp.zeros((POOL_TAIL, POOL_WIDTH), F32)
        hst[...] = jnp.zeros((BATCH, LRU_WIDTH), F32)

    def rmsnorm_to(src, g_ref, dst, cast):
        for r0 in range(0, ROWS, 32):
            rows = pl.ds(r0, 32)
            xv = src[rows, :]
            ms = jnp.mean(xv * xv, axis=-1, keepdims=True)
            dst[rows, :] = ((xv * lax.rsqrt(ms + EPS)) * g_ref[...]).astype(cast)

    rmsnorm_to(x_ref, ng_ref, hb, BF16)
    for r0 in range(0, ROWS, RT):
        rows = pl.ds(r0, RT)
        pbf[rows, :] = p_ref[rows, :].astype(BF16)

    lam = lam_ref[...]
    neg_c_softplus = -LRU_C * (jnp.maximum(-lam, 0.0) + jnp.log1p(jnp.exp(-jnp.abs(lam))))

    for c in range(N_CHUNKS):
        c0 = c * CW
        xa_ext[c, CONV_TAIL:CONV_TAIL + ROWS, :] = _dot(hb[...], win_ref[:, OFF_XA + c0:OFF_XA + c0 + CW])
        buf_d[...] = _dot(hb[...], win_ref[:, OFF_GA + c0:OFF_GA + c0 + CW])
        for r0 in range(0, ROWS, RT):
            rows = pl.ds(r0, RT)
            acc = cb_ref[:, c0:c0 + CW]
            for k in range(CONV_WIDTH):
                acc = acc + xa_ext[c, r0 + k * BATCH:r0 + k * BATCH + RT, :] * cw_ref[k:k + 1, c0:c0 + CW]
            buf_c[rows, :] = acc
            xcb[rows, :] = acc.astype(BF16)
        xa_ext[c, 0:CONV_TAIL, :] = xa_ext[c, ROWS:ROWS + CONV_TAIL, :]
        for hh in range(CW // LRU_HEAD_DIM):
            head = c * (CW // LRU_HEAD_DIM) + hh
            l0 = hh * LRU_HEAD_DIM
            ch0 = c0 + l0
            gbuf[...] = _dot(xcb[:, l0:l0 + LRU_HEAD_DIM], wax_ref[head])
            for r0 in range(0, ROWS, RT):
                rows = pl.ds(r0, RT)
                r_gate = _sigmoid(gbuf[rows, 0:LRU_HEAD_DIM] + ba_ref[:, ch0:ch0 + LRU_HEAD_DIM])
                i_gate = _sigmoid(gbuf[rows, LRU_HEAD_DIM:2 * LRU_HEAD_DIM] + bx_ref[:, ch0:ch0 + LRU_HEAD_DIM])
                log_a = neg_c_softplus[:, ch0:ch0 + LRU_HEAD_DIM] * r_gate
                a = jnp.exp(log_a)
                mult = jnp.sqrt(1.0 - a * a)
                buf_a[rows, l0:l0 + LRU_HEAD_DIM] = a
                buf_b[rows, l0:l0 + LRU_HEAD_DIM] = mult * (i_gate * buf_c[rows, l0:l0 + LRU_HEAD_DIM])
        h = hst[:, c0:c0 + CW]
        for t in range(TT):
            rows = pl.ds(t * BATCH, BATCH)
            h = buf_a[rows, :] * h + buf_b[rows, :]
            ga = buf_d[rows, :]
            ya[rows, c0:c0 + CW] = (h * (ga * _sigmoid(ga))).astype(BF16)
        hst[:, c0:c0 + CW] = h

    xb_ext[POOL_TAIL:POOL_TAIL + ROWS, :] = _dot(hb[...], win_ref[:, OFF_XB:OFF_XB + POOL_WIDTH])
    for half in range(POOL_WIDTH // CW):
        gcols = slice(half * CW, (half + 1) * CW)
        buf_a[...] = _dot(hb[...], win_ref[:, OFF_GB + half * CW:OFF_GB + (half + 1) * CW])
        for gg in range(CW // POOL_GROUP_DIM):
            g = half * (CW // POOL_GROUP_DIM) + gg
            k = POOL_WINDOWS[g]
            lanes = slice(g * POOL_GROUP_DIM, (g + 1) * POOL_GROUP_DIM)
            for r0 in range(0, ROWS, RT):
                cur = xb_ext[POOL_TAIL + r0:POOL_TAIL + r0 + RT, lanes]
                s = cur
                for j in range(1, k):
                    s = s + xb_ext[POOL_TAIL + r0 - j * BATCH:POOL_TAIL + r0 - j * BATCH + RT, lanes]
                row = lax.broadcasted_iota(jnp.int32, (RT, POOL_GROUP_DIM), 0) + r0
                pos = ti * TT + row // BATCH
                cnt = jnp.minimum(pos + 1, k).astype(F32)
                dbuf[r0:r0 + RT, :] = (s / cnt - cur).astype(BF16)
            buf_b[:, 0:POOL_GROUP_DIM] = _dot(dbuf[...], pw_ref[g])
            for r0 in range(0, ROWS, RT):
                rows = pl.ds(r0, RT)
                gb = buf_a[rows, gg * POOL_GROUP_DIM:(gg + 1) * POOL_GROUP_DIM]
                y = buf_b[rows, 0:POOL_GROUP_DIM] * ps_ref[:, lanes]
                yb[rows, lanes] = (y * (gb * _sigmoid(gb))).astype(BF16)
    xb_ext[0:POOL_TAIL, :] = xb_ext[ROWS:ROWS + POOL_TAIL, :]

    for c in range(N_CHUNKS):
        c0 = c * CW
        buf_a[...] = _dot(ya[...], plru_ref[:, c0:c0 + CW])
        buf_b[...] = _dot(yb[...], ppool_ref[:, c0:c0 + CW])
        buf_c[...] = _dot(hb[...], win_ref[:, OFF_MA + c0:OFF_MA + c0 + CW])
        buf_d[...] = _dot(hb[...], win_ref[:, OFF_MB + c0:OFF_MB + c0 + CW])
        for r0 in range(0, ROWS, RT):
            rows = pl.ds(r0, RT)
            m = _sigmoid(buf_c[rows, :]) * buf_a[rows, :] + _sigmoid(buf_d[rows, :]) * buf_b[rows, :]
            mbuf[rows, c0:c0 + CW] = m.astype(BF16)

    for c in range(N_CHUNKS):
        c0 = c * CW
        x1[:, c0:c0 + CW] = x_ref[:, c0:c0 + CW] + _dot(mbuf[...], wout_ref[:, c0:c0 + CW])

    rmsnorm_to(x1, pg_ref, hb, BF16)
    for c in range(N_CHUNKS):
        c0 = c * CW
        buf_a[...] = _dot(hb[...], wpg_ref[:, c0:c0 + CW])
        buf_b[...] = _dot(pbf[...], wpe_ref[:, c0:c0 + CW])
        for r0 in range(0, ROWS, RT):
            rows = pl.ds(r0, RT)
            x1[rows, c0:c0 + CW] = x1[rows, c0:c0 + CW] + _sigmoid(buf_a[rows, :]) * buf_b[rows, :]

    rmsnorm_to(x1, fg_ref, o_ref, F32)


def _const_spec(shape):
    zeros = (0,) * len(shape)
    return pl.BlockSpec(shape, lambda i: zeros, pipeline_mode=pl.Buffered(1))


@jax.jit
def kernel(x, p, norm_g, w_in, conv_w, conv_b, lru_w_a, lru_b_a, lru_w_x, lru_b_x, lru_lambda,
           pool_w, pool_scale, w_proj_lru, w_proj_pool, w_out, ple_norm_g, w_ple_gate,
           w_ple_proj, final_g):
    assert x.shape == (BATCH, SEQ, D_MODEL) and p.shape == (1, BATCH, SEQ, P_DIM)
    xt = jnp.transpose(x, (1, 0, 2)).reshape(SEQ * BATCH, D_MODEL)
    pt = jnp.transpose(p[0], (1, 0, 2)).reshape(SEQ * BATCH, P_DIM)
    row = lambda v: v.reshape(1, -1).astype(F32)
    w_ax = jnp.concatenate([lru_w_a[0], lru_w_x[0]], axis=-1).astype(BF16)
    consts = [
        row(norm_g[0]), w_in[0].astype(BF16), conv_w[0].astype(F32), row(conv_b[0]), w_ax,
        row(lru_b_a[0]), row(lru_b_x[0]), row(lru_lambda[0]), pool_w[0].astype(BF16),
        row(pool_scale[0]), w_proj_lru[0].astype(BF16), w_proj_pool[0].astype(BF16),
        w_out[0].astype(BF16), row(ple_norm_g[0]), w_ple_gate[0].astype(BF16),
        w_ple_proj[0].astype(BF16), row(final_g),
    ]
    in_specs = [
        pl.BlockSpec((ROWS, D_MODEL), lambda i: (i, 0)),
        pl.BlockSpec((ROWS, P_DIM), lambda i: (i, 0)),
    ] + [_const_spec(c.shape) for c in consts]
    scratch = [
        pltpu.VMEM((ROWS, D_MODEL), BF16),
        pltpu.VMEM((ROWS, P_DIM), BF16),
        pltpu.VMEM((N_CHUNKS, ROWS + CONV_TAIL, CW), F32),
        pltpu.VMEM((ROWS + POOL_TAIL, POOL_WIDTH), F32),
        pltpu.VMEM((BATCH, LRU_WIDTH), F32),
        pltpu.VMEM((ROWS, CW), F32),
        pltpu.VMEM((ROWS, CW), F32),
        pltpu.VMEM((ROWS, CW), F32),
        pltpu.VMEM((ROWS, CW), F32),
        pltpu.VMEM((ROWS, CW), BF16),
        pltpu.VMEM((ROWS, LRU_WIDTH), BF16),
        pltpu.VMEM((ROWS, POOL_WIDTH), BF16),
        pltpu.VMEM((ROWS, POOL_GROUP_DIM), BF16),
        pltpu.VMEM((ROWS, CW), F32),
        pltpu.VMEM((ROWS, D_MODEL), BF16),
        pltpu.VMEM((ROWS, D_MODEL), F32),
    ]
    out_t = pl.pallas_call(
        _block_kernel,
        grid=(SEQ // TT,),
        in_specs=in_specs,
        out_specs=pl.BlockSpec((ROWS, D_MODEL), lambda i: (i, 0)),
        out_shape=jax.ShapeDtypeStruct((SEQ * BATCH, D_MODEL), F32),
        scratch_shapes=scratch,
        compiler_params=pltpu.CompilerParams(
            dimension_semantics=("arbitrary",),
            vmem_limit_bytes=VMEM_LIMIT_BYTES,
        ),
        name="rglru_pool_block",
    )(xt, pt, *consts)
    return jnp.transpose(out_t.reshape(SEQ, BATCH, D_MODEL), (1, 0, 2))
```

```python
import functools

import jax
import jax.numpy as jnp
from jax import lax
from jax.experimental import pallas as pl
from jax.experimental.pallas import tpu as pltpu

D_MODEL = 1024
BATCH = 16
SEQ = 2048
P_DIM = 256
LRU_WIDTH = 1024
LRU_HEADS = 8
LRU_HEAD_DIM = 128
CONV_WIDTH = 4
LRU_C = 8.0
POOL_WIDTH = 512
POOL_WINDOWS = (2, 4, 8, 16)
POOL_GROUP_DIM = 128
MAX_WIN = 16
EPS = 1e-6

OFF_XA = 0
OFF_GA = LRU_WIDTH
OFF_XB = 2 * LRU_WIDTH
OFF_GB = OFF_XB + POOL_WIDTH
OFF_MA = OFF_GB + POOL_WIDTH
OFF_MB = OFF_MA + D_MODEL
IN_COLS = OFF_MB + D_MODEL

TT = 32
ROWS = TT * BATCH
CW = 256
N_CHUNKS = D_MODEL // CW
RT = 64
CONV_TAIL = (CONV_WIDTH - 1) * BATCH
POOL_TAIL = MAX_WIN * BATCH
VMEM_LIMIT_BYTES = 56 * 1024 * 1024

F32 = jnp.float32
BF16 = jnp.bfloat16


def _sigmoid(v):
    return 0.5 * jnp.tanh(0.5 * v) + 0.5


def _dot(a, b):
    return jnp.dot(a, b, preferred_element_type=F32)


def _block_kernel(x_hbm, p_hbm, ng_ref, win_ref, cw_ref, cb_ref, wax_ref, ba_ref, bx_ref,
                  lam_ref, pw_ref, ps_ref, plru_ref, ppool_ref, wout_ref, pg_ref, wpg_ref,
                  wpe_ref, fg_ref,
                  o_hbm,
                  xbuf, pbuf, obuf, sem_x, sem_p, sem_o,
                  hb, pbf, xa_ext, xb_ext, hst, buf_a, buf_b, buf_c, buf_d, xcb, ya, yb,
                  dbuf, gbuf, mbuf, x1):
    ti = pl.program_id(0)
    n_steps = pl.num_programs(0)
    slot = ti % 2

    def x_copy(step, sl, b):
        return pltpu.make_async_copy(x_hbm.at[b, pl.ds(step * TT, TT), :], xbuf.at[sl, :, b, :],
                                     sem_x.at[sl, b])

    def p_copy(step, sl, b):
        return pltpu.make_async_copy(p_hbm.at[0, b, pl.ds(step * TT, TT), :], pbuf.at[sl, :, b, :],
                                     sem_p.at[sl, b])

    def o_copy(step, sl, b):
        return pltpu.make_async_copy(obuf.at[sl, :, b, :], o_hbm.at[b, pl.ds(step * TT, TT), :],
                                     sem_o.at[sl, b])

    def start_inputs(step, sl):
        for b in range(BATCH):
            x_copy(step, sl, b).start()
            p_copy(step, sl, b).start()

    @pl.when(ti == 0)
    def _():
        start_inputs(0, 0)
        xa_ext[:, 0:CONV_TAIL, :] = jnp.zeros((N_CHUNKS, CONV_TAIL, CW), F32)
        xb_ext[0:POOL_TAIL, :] = jnp.zeros((POOL_TAIL, POOL_WIDTH), F32)
        hst[...] = jnp.zeros((BATCH, LRU_WIDTH), F32)

    @pl.when(ti + 1 < n_steps)
    def _():
        start_inputs(ti + 1, 1 - slot)

    for b in range(BATCH):
        x_copy(ti, slot, b).wait()
        p_copy(ti, slot, b).wait()

    def x_rows(r0, nrows, c0=0, ncols=D_MODEL):
        t0, nt = r0 // BATCH, nrows // BATCH
        return xbuf[slot, pl.ds(t0, nt), :, c0:c0 + ncols].reshape(nrows, ncols)

    def rmsnorm(xv, g_ref, cast):
        ms = jnp.mean(xv * xv, axis=-1, keepdims=True)
        return ((xv * lax.rsqrt(ms + EPS)) * g_ref[...]).astype(cast)

    for r0 in range(0, ROWS, 32):
        hb[pl.ds(r0, 32), :] = rmsnorm(x_rows(r0, 32), ng_ref, BF16)
    for r0 in range(0, ROWS, RT):
        pv = pbuf[slot, pl.ds(r0 // BATCH, RT // BATCH), :, :].reshape(RT, P_DIM)
        pbf[pl.ds(r0, RT), :] = pv.astype(BF16)

    lam = lam_ref[...]
    neg_c_softplus = -LRU_C * (jnp.maximum(-lam, 0.0) + jnp.log1p(jnp.exp(-jnp.abs(lam))))

    for c in range(N_CHUNKS):
        c0 = c * CW
        xa_ext[c, CONV_TAIL:CONV_TAIL + ROWS, :] = _dot(hb[...], win_ref[:, OFF_XA + c0:OFF_XA + c0 + CW])
        buf_d[...] = _dot(hb[...], win_ref[:, OFF_GA + c0:OFF_GA + c0 + CW])
        for r0 in range(0, ROWS, RT):
            rows = pl.ds(r0, RT)
            acc = cb_ref[:, c0:c0 + CW]
            for k in range(CONV_WIDTH):
                acc = acc + xa_ext[c, r0 + k * BATCH:r0 + k * BATCH + RT, :] * cw_ref[k:k + 1, c0:c0 + CW]
            buf_c[rows, :] = acc
            xcb[rows, :] = acc.astype(BF16)
        xa_ext[c, 0:CONV_TAIL, :] = xa_ext[c, ROWS:ROWS + CONV_TAIL, :]
        for hh in range(CW // LRU_HEAD_DIM):
            head = c * (CW // LRU_HEAD_DIM) + hh
            l0 = hh * LRU_HEAD_DIM
            ch0 = c0 + l0
            gbuf[...] = _dot(xcb[:, l0:l0 + LRU_HEAD_DIM], wax_ref[head])
            for r0 in range(0, ROWS, RT):
                rows = pl.ds(r0, RT)
                r_gate = _sigmoid(gbuf[rows, 0:LRU_HEAD_DIM] + ba_ref[:, ch0:ch0 + LRU_HEAD_DIM])
                i_gate = _sigmoid(gbuf[rows, LRU_HEAD_DIM:2 * LRU_HEAD_DIM] + bx_ref[:, ch0:ch0 + LRU_HEAD_DIM])
                log_a = neg_c_softplus[:, ch0:ch0 + LRU_HEAD_DIM] * r_gate
                a = jnp.exp(log_a)
                mult = jnp.sqrt(1.0 - a * a)
                buf_a[rows, l0:l0 + LRU_HEAD_DIM] = a
                buf_b[rows, l0:l0 + LRU_HEAD_DIM] = mult * (i_gate * buf_c[rows, l0:l0 + LRU_HEAD_DIM])
        h = hst[:, c0:c0 + CW]
        for t in range(TT):
            rows = pl.ds(t * BATCH, BATCH)
            h = buf_a[rows, :] * h + buf_b[rows, :]
            ga = buf_d[rows, :]
            ya[rows, c0:c0 + CW] = (h * (ga * _sigmoid(ga))).astype(BF16)
        hst[:, c0:c0 + CW] = h

    xb_ext[POOL_TAIL:POOL_TAIL + ROWS, :] = _dot(hb[...], win_ref[:, OFF_XB:OFF_XB + POOL_WIDTH])
    for half in range(POOL_WIDTH // CW):
        gcols = slice(half * CW, (half + 1) * CW)
        buf_a[...] = _dot(hb[...], win_ref[:, OFF_GB + half * CW:OFF_GB + (half + 1) * CW])
        for gg in range(CW // POOL_GROUP_DIM):
            g = half * (CW // POOL_GROUP_DIM) + gg
            k = POOL_WINDOWS[g]
            lanes = slice(g * POOL_GROUP_DIM, (g + 1) * POOL_GROUP_DIM)
            for r0 in range(0, ROWS, RT):
                cur = xb_ext[POOL_TAIL + r0:POOL_TAIL + r0 + RT, lanes]
                s = cur
                for j in range(1, k):
                    s = s + xb_ext[POOL_TAIL + r0 - j * BATCH:POOL_TAIL + r0 - j * BATCH + RT, lanes]
                row = lax.broadcasted_iota(jnp.int32, (RT, POOL_GROUP_DIM), 0) + r0
                pos = ti * TT + row // BATCH
                cnt = jnp.minimum(pos + 1, k).astype(F32)
                dbuf[r0:r0 + RT, :] = (s / cnt - cur).astype(BF16)
            buf_b[:, 0:POOL_GROUP_DIM] = _dot(dbuf[...], pw_ref[g])
            for r0 in range(0, ROWS, RT):
                rows = pl.ds(r0, RT)
                gb = buf_a[rows, gg * POOL_GROUP_DIM:(gg + 1) * POOL_GROUP_DIM]
                y = buf_b[rows, 0:POOL_GROUP_DIM] * ps_ref[:, lanes]
                yb[rows, lanes] = (y * (gb * _sigmoid(gb))).astype(BF16)
    xb_ext[0:POOL_TAIL, :] = xb_ext[ROWS:ROWS + POOL_TAIL, :]

    for c in range(N_CHUNKS):
        c0 = c * CW
        buf_a[...] = _dot(ya[...], plru_ref[:, c0:c0 + CW])
        buf_b[...] = _dot(yb[...], ppool_ref[:, c0:c0 + CW])
        buf_c[...] = _dot(hb[...], win_ref[:, OFF_MA + c0:OFF_MA + c0 + CW])
        buf_d[...] = _dot(hb[...], win_ref[:, OFF_MB + c0:OFF_MB + c0 + CW])
        for r0 in range(0, ROWS, RT):
            rows = pl.ds(r0, RT)
            m = _sigmoid(buf_c[rows, :]) * buf_a[rows, :] + _sigmoid(buf_d[rows, :]) * buf_b[rows, :]
            mbuf[rows, c0:c0 + CW] = m.astype(BF16)

    for c in range(N_CHUNKS):
        c0 = c * CW
        x1[:, c0:c0 + CW] = x_rows(0, ROWS, c0, CW) + _dot(mbuf[...], wout_ref[:, c0:c0 + CW])

    for r0 in range(0, ROWS, 32):
        hb[pl.ds(r0, 32), :] = rmsnorm(x1[pl.ds(r0, 32), :], pg_ref, BF16)
    for c in range(N_CHUNKS):
        c0 = c * CW
        buf_a[...] = _dot(hb[...], wpg_ref[:, c0:c0 + CW])
        buf_b[...] = _dot(pbf[...], wpe_ref[:, c0:c0 + CW])
        for r0 in range(0, ROWS, RT):
            rows = pl.ds(r0, RT)
            x1[rows, c0:c0 + CW] = x1[rows, c0:c0 + CW] + _sigmoid(buf_a[rows, :]) * buf_b[rows, :]

    @pl.when(ti >= 2)
    def _():
        for b in range(BATCH):
            o_copy(ti - 2, slot, b).wait()

    for r0 in range(0, ROWS, 32):
        ov = rmsnorm(x1[pl.ds(r0, 32), :], fg_ref, F32)
        obuf[slot, pl.ds(r0 // BATCH, 32 // BATCH), :, :] = ov.reshape(32 // BATCH, BATCH, D_MODEL)

    for b in range(BATCH):
        o_copy(ti, slot, b).start()

    @pl.when(ti == n_steps - 1)
    def _():
        for b in range(BATCH):
            o_copy(ti - 1, 1 - slot, b).wait()
        for b in range(BATCH):
            o_copy(ti, slot, b).wait()


def _const_spec(shape):
    zeros = (0,) * len(shape)
    return pl.BlockSpec(shape, lambda i: zeros, pipeline_mode=pl.Buffered(1))


@jax.jit
def kernel(x, p, norm_g, w_in, conv_w, conv_b, lru_w_a, lru_b_a, lru_w_x, lru_b_x, lru_lambda,
           pool_w, pool_scale, w_proj_lru, w_proj_pool, w_out, ple_norm_g, w_ple_gate,
           w_ple_proj, final_g):
    assert x.shape == (BATCH, SEQ, D_MODEL) and p.shape == (1, BATCH, SEQ, P_DIM)
    row = lambda v: v.reshape(1, -1).astype(F32)
    w_ax = jnp.concatenate([lru_w_a[0], lru_w_x[0]], axis=-1).astype(BF16)
    consts = [
        row(norm_g[0]), w_in[0].astype(BF16), conv_w[0].astype(F32), row(conv_b[0]), w_ax,
        row(lru_b_a[0]), row(lru_b_x[0]), row(lru_lambda[0]), pool_w[0].astype(BF16),
        row(pool_scale[0]), w_proj_lru[0].astype(BF16), w_proj_pool[0].astype(BF16),
        w_out[0].astype(BF16), row(ple_norm_g[0]), w_ple_gate[0].astype(BF16),
        w_ple_proj[0].astype(BF16), row(final_g),
    ]
    in_specs = [
        pl.BlockSpec(memory_space=pl.ANY),
        pl.BlockSpec(memory_space=pl.ANY),
    ] + [_const_spec(c.shape) for c in consts]
    scratch = [
        pltpu.VMEM((2, TT, BATCH, D_MODEL), F32),
        pltpu.VMEM((2, TT, BATCH, P_DIM), F32),
        pltpu.VMEM((2, TT, BATCH, D_MODEL), F32),
        pltpu.SemaphoreType.DMA((2, BATCH)),
        pltpu.SemaphoreType.DMA((2, BATCH)),
        pltpu.SemaphoreType.DMA((2, BATCH)),
        pltpu.VMEM((ROWS, D_MODEL), BF16),
        pltpu.VMEM((ROWS, P_DIM), BF16),
        pltpu.VMEM((N_CHUNKS, ROWS + CONV_TAIL, CW), F32),
        pltpu.VMEM((ROWS + POOL_TAIL, POOL_WIDTH), F32),
        pltpu.VMEM((BATCH, LRU_WIDTH), F32),
        pltpu.VMEM((ROWS, CW), F32),
        pltpu.VMEM((ROWS, CW), F32),
        pltpu.VMEM((ROWS, CW), F32),
        pltpu.VMEM((ROWS, CW), F32),
        pltpu.VMEM((ROWS, CW), BF16),
        pltpu.VMEM((ROWS, LRU_WIDTH), BF16),
        pltpu.VMEM((ROWS, POOL_WIDTH), BF16),
        pltpu.VMEM((ROWS, POOL_GROUP_DIM), BF16),
        pltpu.VMEM((ROWS, CW), F32),
        pltpu.VMEM((ROWS, D_MODEL), BF16),
        pltpu.VMEM((ROWS, D_MODEL), F32),
    ]
    return pl.pallas_call(
        _block_kernel,
        grid=(SEQ // TT,),
        in_specs=in_specs,
        out_specs=pl.BlockSpec(memory_space=pl.ANY),
        out_shape=jax.ShapeDtypeStruct((BATCH, SEQ, D_MODEL), F32),
        scratch_shapes=scratch,
        compiler_params=pltpu.CompilerParams(
            dimension_semantics=("arbitrary",),
            vmem_limit_bytes=VMEM_LIMIT_BYTES,
        ),
        name="rglru_pool_block",
    )(x, p, *consts)
```

```python
import jax
import jax.numpy as jnp
from jax import lax
from jax.experimental import pallas as pl
from jax.experimental.pallas import tpu as pltpu

D_MODEL = 1024
BATCH = 16
SEQ = 2048
P_DIM = 256
LRU_WIDTH = 1024
LRU_HEADS = 8
LRU_HEAD_DIM = 128
CONV_WIDTH = 4
LRU_C = 8.0
POOL_WIDTH = 512
POOL_WINDOWS = (2, 4, 8, 16)
POOL_GROUPS = 4
POOL_GROUP_DIM = 128
MAX_WIN = 16
EPS = 1e-6

OFF_XA = 0
OFF_GA = LRU_WIDTH
OFF_XB = 2 * LRU_WIDTH
OFF_GB = OFF_XB + POOL_WIDTH
OFF_MA = OFF_GB + POOL_WIDTH
OFF_MB = OFF_MA + D_MODEL
IN_COLS = OFF_MB + D_MODEL

TT = 32
ROWS = TT * BATCH
CW = 256
N_CHUNKS = D_MODEL // CW
HEADS_PER_CHUNK = CW // LRU_HEAD_DIM
RT = 64
NORM_RT = 32
CONV_TAIL = (CONV_WIDTH - 1) * BATCH
POOL_TAIL = MAX_WIN * BATCH
VMEM_LIMIT_BYTES = 58 * 1024 * 1024

TB_GA = 0
TB_XC, TB_G0, TB_G1, TB_A, TB_U = 4, 6, 8, 10, 12
TB_MA, TB_MB = 14, 18
TB_GB = 22
N_TB = 24
TB_PA, TB_PB, TB_GATE, TB_PE = 0, 2, 4, 6

F32 = jnp.float32
BF16 = jnp.bfloat16
TINY = 1e-30


def _sigmoid(v):
    return 0.5 * jnp.tanh(0.5 * v) + 0.5


def _dot(a, b):
    return jnp.dot(a, b, preferred_element_type=F32)


def _block_kernel(x_hbm, p_hbm, ng_ref, win_ref, cw_ref, cb_ref, wax_ref, ba_ref, bx_ref,
                  lam_ref, pw_ref, ps_ref, plru_ref, ppool_ref, wout_ref, pg_ref, wpg_ref,
                  wpe_ref, fg_ref,
                  o_hbm,
                  xbuf, pbuf, obuf, sem_x, sem_p, sem_o,
                  hb, pbf, xa_ext, xb_ext, hst, tb, xcb, ya, yb, dbuf, ybuf, mbuf, x1):
    ti = pl.program_id(0)
    n_steps = pl.num_programs(0)
    slot = ti % 2

    def x_copy(step, sl, b):
        return pltpu.make_async_copy(x_hbm.at[b, pl.ds(step * TT, TT), :], xbuf.at[sl, :, b, :],
                                     sem_x.at[sl, b])

    def p_copy(step, sl, b):
        return pltpu.make_async_copy(p_hbm.at[0, b, pl.ds(step * TT, TT), :], pbuf.at[sl, :, b, :],
                                     sem_p.at[sl, b])

    def o_copy(step, sl, b):
        return pltpu.make_async_copy(obuf.at[sl, :, b, :], o_hbm.at[b, pl.ds(step * TT, TT), :],
                                     sem_o.at[sl, b])

    def start_inputs(step, sl):
        for b in range(BATCH):
            x_copy(step, sl, b).start()
            p_copy(step, sl, b).start()

    @pl.when(ti == 0)
    def _():
        start_inputs(0, 0)
        xa_ext[:, 0:CONV_TAIL, :] = jnp.zeros((N_CHUNKS, CONV_TAIL, CW), F32)
        xb_ext[0:POOL_TAIL, :] = jnp.zeros((POOL_TAIL, POOL_WIDTH), F32)
        hst[...] = jnp.zeros((BATCH, LRU_WIDTH), F32)

    @pl.when(ti + 1 < n_steps)
    def _():
        start_inputs(ti + 1, 1 - slot)

    @pl.when(ti >= 2)
    def _():
        for b in range(BATCH):
            o_copy(ti - 2, slot, b).wait()

    for b in range(BATCH):
        x_copy(ti, slot, b).wait()
        p_copy(ti, slot, b).wait()

    def x_rows(r0, nrows, c0=0, ncols=D_MODEL):
        t0, nt = r0 // BATCH, nrows // BATCH
        return xbuf[slot, pl.ds(t0, nt), :, c0:c0 + ncols].reshape(nrows, ncols)

    def rmsnorm(xv, g_ref, cast):
        ms = jnp.mean(xv * xv, axis=-1, keepdims=True)
        return ((xv * lax.rsqrt(ms + EPS)) * g_ref[...]).astype(cast)

    def win(off, c):
        return win_ref[:, off + c * CW:off + (c + 1) * CW]

    for r0 in range(0, ROWS, RT):
        pv = pbuf[slot, pl.ds(r0 // BATCH, RT // BATCH), :, :].reshape(RT, P_DIM)
        pbf[pl.ds(r0, RT), :] = pv.astype(BF16)
    for r0 in range(0, ROWS, NORM_RT):
        hb[pl.ds(r0, NORM_RT), :] = rmsnorm(x_rows(r0, NORM_RT), ng_ref, BF16)

    lam = lam_ref[...]
    neg_c_softplus = -LRU_C * (jnp.maximum(-lam, 0.0) + jnp.log1p(jnp.exp(-jnp.abs(lam))))

    def dots_xa_ga(c):
        xa_ext[c, CONV_TAIL:CONV_TAIL + ROWS, :] = _dot(hb[...], win(OFF_XA, c))
        tb[TB_GA + c] = _dot(hb[...], win(OFF_GA, c))

    def conv(c):
        q, c0 = c % 2, c * CW
        for r0 in range(0, ROWS, RT):
            rows = pl.ds(r0, RT)
            acc = cb_ref[:, c0:c0 + CW]
            for k in range(CONV_WIDTH):
                acc = acc + xa_ext[c, r0 + k * BATCH:r0 + k * BATCH + RT, :] * cw_ref[k:k + 1, c0:c0 + CW]
            tb[TB_XC + q, rows, :] = acc
            xcb[q, rows, :] = acc.astype(BF16)
        xa_ext[c, 0:CONV_TAIL, :] = xa_ext[c, ROWS:ROWS + CONV_TAIL, :]

    def dots_gates(c):
        q = c % 2
        for hh, tg in enumerate((TB_G0, TB_G1)):
            l0 = hh * LRU_HEAD_DIM
            tb[tg + q] = _dot(xcb[q, :, l0:l0 + LRU_HEAD_DIM], wax_ref[c * HEADS_PER_CHUNK + hh])

    def gates(c):
        q = c % 2
        for hh, tg in enumerate((TB_G0, TB_G1)):
            l0 = hh * LRU_HEAD_DIM
            ch = slice(c * CW + l0, c * CW + l0 + LRU_HEAD_DIM)
            lanes = slice(l0, l0 + LRU_HEAD_DIM)
            for r0 in range(0, ROWS, RT):
                rows = pl.ds(r0, RT)
                r_gate = _sigmoid(tb[tg + q, rows, 0:LRU_HEAD_DIM] + ba_ref[:, ch])
                i_gate = _sigmoid(tb[tg + q, rows, LRU_HEAD_DIM:2 * LRU_HEAD_DIM] + bx_ref[:, ch])
                a = jnp.exp(neg_c_softplus[:, ch] * r_gate)
                y = 1.0 - a * a
                mult = y * lax.rsqrt(jnp.maximum(y, TINY))
                tb[TB_A + q, rows, lanes] = a
                tb[TB_U + q, rows, lanes] = mult * (i_gate * tb[TB_XC + q, rows, lanes])

    def scan(c):
        q, c0 = c % 2, c * CW
        h = hst[:, c0:c0 + CW]
        for t in range(TT):
            rows = pl.ds(t * BATCH, BATCH)
            h = tb[TB_A + q, rows, :] * h + tb[TB_U + q, rows, :]
            ga = tb[TB_GA + c, rows, :]
            ya[rows, c0:c0 + CW] = (h * (ga * _sigmoid(ga))).astype(BF16)
        hst[:, c0:c0 + CW] = h

    def dots_pool_in():
        xb_ext[POOL_TAIL:POOL_TAIL + ROWS, :] = _dot(hb[...], win_ref[:, OFF_XB:OFF_XB + POOL_WIDTH])
        for half in range(POOL_WIDTH // CW):
            tb[TB_GB + half] = _dot(hb[...], win(OFF_GB, half))

    def pool_windows(g):
        k = POOL_WINDOWS[g]
        lanes = slice(g * POOL_GROUP_DIM, (g + 1) * POOL_GROUP_DIM)
        for r0 in range(0, ROWS, RT):
            cur = xb_ext[POOL_TAIL + r0:POOL_TAIL + r0 + RT, lanes]
            s = cur
            for j in range(1, k):
                s = s + xb_ext[POOL_TAIL + r0 - j * BATCH:POOL_TAIL + r0 - j * BATCH + RT, lanes]
            row = lax.broadcasted_iota(jnp.int32, (RT, POOL_GROUP_DIM), 0) + r0
            pos = ti * TT + row // BATCH
            cnt = jnp.minimum(pos + 1, k).astype(F32)
            dbuf[g, r0:r0 + RT, :] = (s / cnt - cur).astype(BF16)

    def dot_pool(g):
        ybuf[g] = _dot(dbuf[g], pw_ref[g])

    def pool_out(g):
        lanes = slice(g * POOL_GROUP_DIM, (g + 1) * POOL_GROUP_DIM)
        half, gg = divmod(g, CW // POOL_GROUP_DIM)
        for r0 in range(0, ROWS, RT):
            rows = pl.ds(r0, RT)
            gb = tb[TB_GB + half, rows, gg * POOL_GROUP_DIM:(gg + 1) * POOL_GROUP_DIM]
            y = ybuf[g, rows, :] * ps_ref[:, lanes]
            yb[rows, lanes] = (y * (gb * _sigmoid(gb))).astype(BF16)

    def dots_merge_logits(c):
        tb[TB_MA + c] = _dot(hb[...], win(OFF_MA, c))
        tb[TB_MB + c] = _dot(hb[...], win(OFF_MB, c))

    dots_xa_ga(0)
    dots_xa_ga(1)
    for c in range(N_CHUNKS):
        conv(c)
        dots_gates(c)
        if c + 2 < N_CHUNKS:
            dots_xa_ga(c + 2)
        if c == 0:
            dots_pool_in()
        else:
            dots_merge_logits(c - 1)
        gates(c)
        scan(c)
        pool_windows(c)
        dot_pool(c)
    dots_merge_logits(N_CHUNKS - 1)
    xb_ext[0:POOL_TAIL, :] = xb_ext[ROWS:ROWS + POOL_TAIL, :]
    for g in range(POOL_GROUPS):
        pool_out(g)

    def dots_proj(c):
        c0 = c * CW
        tb[TB_PA + c % 2] = _dot(ya[...], plru_ref[:, c0:c0 + CW])
        tb[TB_PB + c % 2] = _dot(yb[...], ppool_ref[:, c0:c0 + CW])

    dots_proj(0)
    for c in range(N_CHUNKS):
        q, c0 = c % 2, c * CW
        if c + 1 < N_CHUNKS:
            dots_proj(c + 1)
        for r0 in range(0, ROWS, RT):
            rows = pl.ds(r0, RT)
            m = (_sigmoid(tb[TB_MA + c, rows, :]) * tb[TB_PA + q, rows, :]
                 + _sigmoid(tb[TB_MB + c, rows, :]) * tb[TB_PB + q, rows, :])
            mbuf[rows, c0:c0 + CW] = m.astype(BF16)

    for c in range(N_CHUNKS):
        c0 = c * CW
        x1[:, c0:c0 + CW] = x_rows(0, ROWS, c0, CW) + _dot(mbuf[...], wout_ref[:, c0:c0 + CW])
    for c in range(N_CHUNKS):
        tb[TB_PE + c] = _dot(pbf[...], wpe_ref[:, c * CW:(c + 1) * CW])

    for r0 in range(0, ROWS, NORM_RT):
        hb[pl.ds(r0, NORM_RT), :] = rmsnorm(x1[pl.ds(r0, NORM_RT), :], pg_ref, BF16)

    def dot_gate(c):
        tb[TB_GATE + c % 2] = _dot(hb[...], wpg_ref[:, c * CW:(c + 1) * CW])

    dot_gate(0)
    for c in range(N_CHUNKS):
        q, c0 = c % 2, c * CW
        if c + 1 < N_CHUNKS:
            dot_gate(c + 1)
        for r0 in range(0, ROWS, RT):
            rows = pl.ds(r0, RT)
            x1[rows, c0:c0 + CW] = (x1[rows, c0:c0 + CW]
                                    + _sigmoid(tb[TB_GATE + q, rows, :]) * tb[TB_PE + c, rows, :])

    for r0 in range(0, ROWS, NORM_RT):
        ov = rmsnorm(x1[pl.ds(r0, NORM_RT), :], fg_ref, F32)
        obuf[slot, pl.ds(r0 // BATCH, NORM_RT // BATCH), :, :] = ov.reshape(NORM_RT // BATCH, BATCH, D_MODEL)

    for b in range(BATCH):
        o_copy(ti, slot, b).start()

    @pl.when(ti == n_steps - 1)
    def _():
        for b in range(BATCH):
            o_copy(ti - 1, 1 - slot, b).wait()
        for b in range(BATCH):
            o_copy(ti, slot, b).wait()


def _const_spec(shape):
    zeros = (0,) * len(shape)
    return pl.BlockSpec(shape, lambda i: zeros, pipeline_mode=pl.Buffered(1))


@jax.jit
def kernel(x, p, norm_g, w_in, conv_w, conv_b, lru_w_a, lru_b_a, lru_w_x, lru_b_x, lru_lambda,
           pool_w, pool_scale, w_proj_lru, w_proj_pool, w_out, ple_norm_g, w_ple_gate,
           w_ple_proj, final_g):
    assert x.shape == (BATCH, SEQ, D_MODEL) and p.shape == (1, BATCH, SEQ, P_DIM)
    row = lambda v: v.reshape(1, -1).astype(F32)
    w_ax = jnp.concatenate([lru_w_a[0], lru_w_x[0]], axis=-1).astype(BF16)
    consts = [
        row(norm_g[0]), w_in[0].astype(BF16), conv_w[0].astype(F32), row(conv_b[0]), w_ax,
        row(lru_b_a[0]), row(lru_b_x[0]), row(lru_lambda[0]), pool_w[0].astype(BF16),
        row(pool_scale[0]), w_proj_lru[0].astype(BF16), w_proj_pool[0].astype(BF16),
        w_out[0].astype(BF16), row(ple_norm_g[0]), w_ple_gate[0].astype(BF16),
        w_ple_proj[0].astype(BF16), row(final_g),
    ]
    in_specs = [
        pl.BlockSpec(memory_space=pl.ANY),
        pl.BlockSpec(memory_space=pl.ANY),
    ] + [_const_spec(c.shape) for c in consts]
    scratch = [
        pltpu.VMEM((2, TT, BATCH, D_MODEL), F32),
        pltpu.VMEM((2, TT, BATCH, P_DIM), F32),
        pltpu.VMEM((2, TT, BATCH, D_MODEL), F32),
        pltpu.SemaphoreType.DMA((2, BATCH)),
        pltpu.SemaphoreType.DMA((2, BATCH)),
        pltpu.SemaphoreType.DMA((2, BATCH)),
        pltpu.VMEM((ROWS, D_MODEL), BF16),
        pltpu.VMEM((ROWS, P_DIM), BF16),
        pltpu.VMEM((N_CHUNKS, ROWS + CONV_TAIL, CW), F32),
        pltpu.VMEM((ROWS + POOL_TAIL, POOL_WIDTH), F32),
        pltpu.VMEM((BATCH, LRU_WIDTH), F32),
        pltpu.VMEM((N_TB, ROWS, CW), F32),
        pltpu.VMEM((2, ROWS, CW), BF16),
        pltpu.VMEM((ROWS, LRU_WIDTH), BF16),
        pltpu.VMEM((ROWS, POOL_WIDTH), BF16),
        pltpu.VMEM((POOL_GROUPS, ROWS, POOL_GROUP_DIM), BF16),
        pltpu.VMEM((POOL_GROUPS, ROWS, POOL_GROUP_DIM), F32),
        pltpu.VMEM((ROWS, D_MODEL), BF16),
        pltpu.VMEM((ROWS, D_MODEL), F32),
    ]
    return pl.pallas_call(
        _block_kernel,
        grid=(SEQ // TT,),
        in_specs=in_specs,
        out_specs=pl.BlockSpec(memory_space=pl.ANY),
        out_shape=jax.ShapeDtypeStruct((BATCH, SEQ, D_MODEL), F32),
        scratch_shapes=scratch,
        compiler_params=pltpu.CompilerParams(
            dimension_semantics=("arbitrary",),
            vmem_limit_bytes=VMEM_LIMIT_BYTES,
        ),
        name="rglru_pool_block",
    )(x, p, *consts)
```

```python
import jax
import jax.numpy as jnp
from jax import lax
from jax.experimental import pallas as pl
from jax.experimental.pallas import tpu as pltpu

D_MODEL = 1024
BATCH = 16
SEQ = 2048
P_DIM = 256
LRU_WIDTH = 1024
LRU_HEADS = 8
LRU_HEAD_DIM = 128
CONV_WIDTH = 4
LRU_C = 8.0
POOL_WIDTH = 512
POOL_WINDOWS = (2, 4, 8, 16)
POOL_GROUPS = 4
POOL_GROUP_DIM = 128
MAX_WIN = 16
EPS = 1e-6

OFF_XA = 0
OFF_GA = LRU_WIDTH
OFF_XB = 2 * LRU_WIDTH
OFF_GB = OFF_XB + POOL_WIDTH
OFF_MA = OFF_GB + POOL_WIDTH
OFF_MB = OFF_MA + D_MODEL
IN_COLS = OFF_MB + D_MODEL

TT = 32
ROWS = TT * BATCH
CW = 256
N_CHUNKS = D_MODEL // CW
HEADS_PER_CHUNK = CW // LRU_HEAD_DIM
RT = 64
NORM_RT = 32
CONV_TAIL = (CONV_WIDTH - 1) * BATCH
POOL_TAIL = MAX_WIN * BATCH
X_SLOTS = 3
VMEM_LIMIT_BYTES = 58 * 1024 * 1024

TB_GA = 0
TB_XC, TB_G0, TB_G1, TB_A, TB_U = 4, 6, 8, 10, 12
TB_MA, TB_MB = 14, 18
TB_GB = 22
N_TB = 24
TB_PA, TB_PB, TB_PE, TB_GATE = 0, 2, 6, 10

F32 = jnp.float32
BF16 = jnp.bfloat16
TINY = 1e-30


def _tanh_p1(vh):
    return jnp.tanh(vh) + 1.0


def _dot(a, b):
    return jnp.dot(a, b, preferred_element_type=F32)


def _block_kernel(x_hbm, p_hbm, ng_ref, win_hbm, cw_ref, cb_ref, wa_hbm, ba_ref, wx_hbm, bx_ref,
                  lam_ref, pw_hbm, ps_ref, plru_hbm, ppool_hbm, wout_hbm, pg_ref, wpg_hbm,
                  wpe_hbm, fg_ref,
                  o_hbm,
                  xbuf, pbuf, obuf, sem_x, sem_p, sem_o, sem_w,
                  win_ref, wax_ref, pw_ref, plru_ref, ppool_ref, wout_ref, wpg_ref, wpe_ref,
                  hb_in, hb2, pbf, xa_ext, xb_ext, hst, tb, xcb, ya, yb, dbuf, ybuf, mbuf, x1):
    ti = pl.program_id(0)
    n_steps = pl.num_programs(0)
    slot = ti % 2
    nxt = jnp.minimum(ti + 1, n_steps - 1)
    nxt2 = jnp.minimum(ti + 2, n_steps - 1)
    nslot = 1 - slot
    xs, xs1, xs2 = ti % X_SLOTS, (ti + 1) % X_SLOTS, (ti + 2) % X_SLOTS

    def x_copy(step, sl, b):
        return pltpu.make_async_copy(x_hbm.at[b, pl.ds(step * TT, TT), :], xbuf.at[sl, :, b, :],
                                     sem_x.at[sl])

    def p_copy(step, sl, b):
        return pltpu.make_async_copy(p_hbm.at[0, b, pl.ds(step * TT, TT), :], pbuf.at[sl, :, b, :],
                                     sem_p.at[sl])

    def o_copy(step, sl, b):
        return pltpu.make_async_copy(obuf.at[sl, :, b, :], o_hbm.at[b, pl.ds(step * TT, TT), :],
                                     sem_o.at[sl])

    def start_copies(copy, step, sl):
        for b in range(BATCH):
            copy(step, sl, b).start()

    def wait_slot(buf, sem, sl):
        pltpu.make_async_copy(buf.at[sl], buf.at[sl], sem.at[sl]).wait()

    def x_rows(sl, r0, nrows, c0=0, ncols=D_MODEL):
        t0, nt = r0 // BATCH, nrows // BATCH
        return xbuf[sl, pl.ds(t0, nt), :, c0:c0 + ncols].reshape(nrows, ncols)

    def rmsnorm(xv, g_ref, cast):
        ms = jnp.mean(xv * xv, axis=-1, keepdims=True)
        return ((xv * lax.rsqrt(ms + EPS)) * g_ref[...]).astype(cast)

    def win(off, c):
        return win_ref[:, off + c * CW:off + (c + 1) * CW]

    def norm_in(xsl, sl):
        for r0 in range(0, ROWS, NORM_RT):
            hb_in[sl, pl.ds(r0, NORM_RT), :] = rmsnorm(x_rows(xsl, r0, NORM_RT), ng_ref, BF16)

    def dots_xa_ga(sl, c):
        xa_ext[c, CONV_TAIL:CONV_TAIL + ROWS, :] = _dot(hb_in[sl], win(OFF_XA, c))
        tb[TB_GA + c] = _dot(hb_in[sl], win(OFF_GA, c))

    def weight_tiles():
        tiles = []

        def add(src, dst, nrows, ncols, scale):
            for r0 in range(0, nrows, ROWS):
                nr = min(ROWS, nrows - r0)
                for c0 in range(0, ncols, CW):
                    tiles.append((src.at[0, pl.ds(r0, nr), pl.ds(c0, CW)], nr, CW,
                                  dst.at[pl.ds(r0, nr), pl.ds(c0, CW)],
                                  scale(c0) if callable(scale) else scale))

        halved_in = lambda c0: 0.5 if (OFF_GA <= c0 < OFF_XB or c0 >= OFF_GB) else 1.0
        add(win_hbm, win_ref, D_MODEL, IN_COLS, halved_in)
        add(plru_hbm, plru_ref, LRU_WIDTH, D_MODEL, 0.5)
        add(ppool_hbm, ppool_ref, POOL_WIDTH, D_MODEL, 0.5)
        add(wout_hbm, wout_ref, D_MODEL, D_MODEL, 1.0)
        add(wpg_hbm, wpg_ref, D_MODEL, D_MODEL, 0.5)
        add(wpe_hbm, wpe_ref, P_DIM, D_MODEL, 0.5)
        for h in range(LRU_HEADS):
            for part, src in enumerate((wa_hbm, wx_hbm)):
                tiles.append((src.at[0, h], LRU_HEAD_DIM, LRU_HEAD_DIM,
                              wax_ref.at[h, :, pl.ds(part * LRU_HEAD_DIM, LRU_HEAD_DIM)], 0.5))
        for g in range(POOL_GROUPS):
            tiles.append((pw_hbm.at[0, g], POOL_GROUP_DIM, POOL_GROUP_DIM, pw_ref.at[g], 1.0))
        return tiles

    def load_weights():
        tiles = weight_tiles()
        for w0 in range(0, len(tiles), N_TB):
            wave = tiles[w0:w0 + N_TB]
            copies = [pltpu.make_async_copy(src, tb.at[k, pl.ds(0, nr), pl.ds(0, nc)], sem_w.at[k])
                      for k, (src, nr, nc, _, _) in enumerate(wave)]
            for cp in copies:
                cp.start()
            for k, (cp, (_, nr, nc, dst, scale)) in enumerate(zip(copies, wave)):
                cp.wait()
                v = tb[k, 0:nr, 0:nc]
                dst[...] = (v if scale == 1.0 else v * scale).astype(BF16)

    @pl.when(ti == 0)
    def _():
        start_copies(x_copy, 0, 0)
        start_copies(x_copy, 1, 1)
        start_copies(p_copy, 0, 0)
        load_weights()
        xa_ext[:, 0:CONV_TAIL, :] = jnp.zeros((N_CHUNKS, CONV_TAIL, CW), F32)
        xb_ext[0:POOL_TAIL, :] = jnp.zeros((POOL_TAIL, POOL_WIDTH), F32)
        hst[...] = jnp.zeros((BATCH, LRU_WIDTH), F32)
        wait_slot(xbuf, sem_x, 0)
        norm_in(0, 0)
        dots_xa_ga(0, 0)
        dots_xa_ga(0, 1)

    @pl.when(ti >= 1)
    def _():
        start_copies(o_copy, ti - 1, nslot)

    @pl.when(ti >= 2)
    def _():
        wait_slot(obuf, sem_o, slot)

    wait_slot(pbuf, sem_p, slot)
    wait_slot(xbuf, sem_x, xs1)

    hb = hb_in.at[slot]
    for r0 in range(0, ROWS, RT):
        pv = pbuf[slot, pl.ds(r0 // BATCH, RT // BATCH), :, :].reshape(RT, P_DIM)
        pbf[pl.ds(r0, RT), :] = pv.astype(BF16)

    lam = lam_ref[...]
    half_c = (-0.5 * LRU_C) * (jnp.maximum(-lam, 0.0) + jnp.log1p(jnp.exp(-jnp.abs(lam))))

    def conv(c):
        q, c0 = c % 2, c * CW
        for r0 in range(0, ROWS, RT):
            rows = pl.ds(r0, RT)
            acc = cb_ref[:, c0:c0 + CW]
            for k in range(CONV_WIDTH):
                acc = acc + xa_ext[c, r0 + k * BATCH:r0 + k * BATCH + RT, :] * cw_ref[k:k + 1, c0:c0 + CW]
            tb[TB_XC + q, rows, :] = acc
            xcb[q, rows, :] = acc.astype(BF16)
        xa_ext[c, 0:CONV_TAIL, :] = xa_ext[c, ROWS:ROWS + CONV_TAIL, :]

    def dots_gates(c):
        q = c % 2
        for hh, tg in enumerate((TB_G0, TB_G1)):
            l0 = hh * LRU_HEAD_DIM
            tb[tg + q] = _dot(xcb[q, :, l0:l0 + LRU_HEAD_DIM], wax_ref[c * HEADS_PER_CHUNK + hh])

    def gates(c):
        q = c % 2
        for hh, tg in enumerate((TB_G0, TB_G1)):
            l0 = hh * LRU_HEAD_DIM
            ch = slice(c * CW + l0, c * CW + l0 + LRU_HEAD_DIM)
            lanes = slice(l0, l0 + LRU_HEAD_DIM)
            for r0 in range(0, ROWS, RT):
                rows = pl.ds(r0, RT)
                r_tanh = jnp.tanh(tb[tg + q, rows, 0:LRU_HEAD_DIM] + 0.5 * ba_ref[:, ch])
                i_gate = 0.5 * jnp.tanh(tb[tg + q, rows, LRU_HEAD_DIM:2 * LRU_HEAD_DIM] + 0.5 * bx_ref[:, ch]) + 0.5
                a = jnp.exp(half_c[:, ch] * r_tanh + half_c[:, ch])
                y = 1.0 - a * a
                mult = y * lax.rsqrt(jnp.maximum(y, TINY))
                tb[TB_A + q, rows, lanes] = a
                tb[TB_U + q, rows, lanes] = mult * (i_gate * tb[TB_XC + q, rows, lanes])

    def scan(c):
        q, c0 = c % 2, c * CW
        h = hst[:, c0:c0 + CW]
        for t in range(TT):
            rows = pl.ds(t * BATCH, BATCH)
            h = tb[TB_A + q, rows, :] * h + tb[TB_U + q, rows, :]
            ga = tb[TB_GA + c, rows, :]
            ya[rows, c0:c0 + CW] = (h * (_tanh_p1(ga) * ga)).astype(BF16)
        hst[:, c0:c0 + CW] = h

    def dots_pool_in():
        xb_ext[POOL_TAIL:POOL_TAIL + ROWS, :] = _dot(hb[...], win_ref[:, OFF_XB:OFF_XB + POOL_WIDTH])
        for half in range(POOL_WIDTH // CW):
            tb[TB_GB + half] = _dot(hb[...], win(OFF_GB, half))

    def pool_windows(g):
        k = POOL_WINDOWS[g]
        lanes = slice(g * POOL_GROUP_DIM, (g + 1) * POOL_GROUP_DIM)
        for r0 in range(0, ROWS, RT):
            cur = xb_ext[POOL_TAIL + r0:POOL_TAIL + r0 + RT, lanes]
            s = cur
            for j in range(1, k):
                s = s + xb_ext[POOL_TAIL + r0 - j * BATCH:POOL_TAIL + r0 - j * BATCH + RT, lanes]
            row = lax.broadcasted_iota(jnp.int32, (RT, POOL_GROUP_DIM), 0) + r0
            pos = ti * TT + row // BATCH
            cnt = jnp.minimum(pos + 1, k).astype(F32)
            dbuf[g, r0:r0 + RT, :] = (s / cnt - cur).astype(BF16)

    def dot_pool(g):
        ybuf[g] = _dot(dbuf[g], pw_ref[g])

    def pool_out(g):
        lanes = slice(g * POOL_GROUP_DIM, (g + 1) * POOL_GROUP_DIM)
        half, gg = divmod(g, CW // POOL_GROUP_DIM)
        for r0 in range(0, ROWS, RT):
            rows = pl.ds(r0, RT)
            gb = tb[TB_GB + half, rows, gg * POOL_GROUP_DIM:(gg + 1) * POOL_GROUP_DIM]
            y = ybuf[g, rows, :] * ps_ref[:, lanes]
            yb[rows, lanes] = (y * (_tanh_p1(gb) * gb)).astype(BF16)

    def dots_merge_logits(c):
        tb[TB_MA + c] = _dot(hb[...], win(OFF_MA, c))
        tb[TB_MB + c] = _dot(hb[...], win(OFF_MB, c))

    for c in range(N_CHUNKS):
        conv(c)
        dots_gates(c)
        if c + 2 < N_CHUNKS:
            dots_xa_ga(slot, c + 2)
        if c == 0:
            dots_pool_in()
        else:
            dots_merge_logits(c - 1)
        gates(c)
        scan(c)
        pool_windows(c)
        dot_pool(c)
    dots_merge_logits(N_CHUNKS - 1)
    xb_ext[0:POOL_TAIL, :] = xb_ext[ROWS:ROWS + POOL_TAIL, :]
    for g in range(POOL_GROUPS):
        pool_out(g)

    def dots_proj(c):
        c0 = c * CW
        tb[TB_PA + c % 2] = _dot(ya[...], plru_ref[:, c0:c0 + CW])
        tb[TB_PB + c % 2] = _dot(yb[...], ppool_ref[:, c0:c0 + CW])

    dots_proj(0)
    for c in range(N_CHUNKS):
        q, c0 = c % 2, c * CW
        if c + 1 < N_CHUNKS:
            dots_proj(c + 1)
        for r0 in range(0, ROWS, RT):
            rows = pl.ds(r0, RT)
            m = (_tanh_p1(tb[TB_MA + c, rows, :]) * tb[TB_PA + q, rows, :]
                 + _tanh_p1(tb[TB_MB + c, rows, :]) * tb[TB_PB + q, rows, :])
            mbuf[rows, c0:c0 + CW] = m.astype(BF16)

    for c in range(N_CHUNKS):
        c0 = c * CW
        x1[:, c0:c0 + CW] = x_rows(xs, 0, ROWS, c0, CW) + _dot(mbuf[...], wout_ref[:, c0:c0 + CW])
    norm_in(xs1, nslot)
    start_copies(x_copy, nxt2, xs2)
    start_copies(p_copy, nxt, nslot)

    for c in range(N_CHUNKS):
        tb[TB_PE + c] = _dot(pbf[...], wpe_ref[:, c * CW:(c + 1) * CW])
    for r0 in range(0, ROWS, NORM_RT):
        hb2[pl.ds(r0, NORM_RT), :] = rmsnorm(x1[pl.ds(r0, NORM_RT), :], pg_ref, BF16)

    def dot_gate(c):
        tb[TB_GATE + c] = _dot(hb2[...], wpg_ref[:, c * CW:(c + 1) * CW])

    dot_gate(0)
    dot_gate(1)
    for c in range(N_CHUNKS):
        q, c0 = c % 2, c * CW
        if c + 2 < N_CHUNKS:
            dot_gate(c + 2)
        else:
            dots_xa_ga(nslot, c + 2 - N_CHUNKS)
        for r0 in range(0, ROWS, RT):
            rows = pl.ds(r0, RT)
            x1[rows, c0:c0 + CW] = (x1[rows, c0:c0 + CW]
                                    + _tanh_p1(tb[TB_GATE + c, rows, :]) * tb[TB_PE + c, rows, :])

    for r0 in range(0, ROWS, NORM_RT):
        ov = rmsnorm(x1[pl.ds(r0, NORM_RT), :], fg_ref, F32)
        obuf[slot, pl.ds(r0 // BATCH, NORM_RT // BATCH), :, :] = ov.reshape(NORM_RT // BATCH, BATCH, D_MODEL)

    @pl.when(ti == n_steps - 1)
    def _():
        start_copies(o_copy, ti, slot)
        wait_slot(xbuf, sem_x, xs2)
        wait_slot(pbuf, sem_p, nslot)
        wait_slot(obuf, sem_o, nslot)
        wait_slot(obuf, sem_o, slot)


def _const_spec(shape):
    zeros = (0,) * len(shape)
    return pl.BlockSpec(shape, lambda i: zeros, pipeline_mode=pl.Buffered(1))


@jax.jit
def kernel(x, p, norm_g, w_in, conv_w, conv_b, lru_w_a, lru_b_a, lru_w_x, lru_b_x, lru_lambda,
           pool_w, pool_scale, w_proj_lru, w_proj_pool, w_out, ple_norm_g, w_ple_gate,
           w_ple_proj, final_g):
    assert x.shape == (BATCH, SEQ, D_MODEL) and p.shape == (1, BATCH, SEQ, P_DIM)
    row = lambda v: v.reshape(1, -1)
    operands = [
        x, p, row(norm_g[0]), w_in, conv_w[0], row(conv_b[0]), lru_w_a, row(lru_b_a[0]), lru_w_x,
        row(lru_b_x[0]), row(lru_lambda[0]), pool_w, row(pool_scale[0]), w_proj_lru, w_proj_pool,
        w_out, row(ple_norm_g[0]), w_ple_gate, w_ple_proj, row(final_g),
    ]
    in_specs = [pl.BlockSpec(memory_space=pl.ANY) if v.ndim > 2 else _const_spec(v.shape)
                for v in operands]
    scratch = [
        pltpu.VMEM((X_SLOTS, TT, BATCH, D_MODEL), F32),
        pltpu.VMEM((2, TT, BATCH, P_DIM), F32),
        pltpu.VMEM((2, TT, BATCH, D_MODEL), F32),
        pltpu.SemaphoreType.DMA((X_SLOTS,)),
        pltpu.SemaphoreType.DMA((2,)),
        pltpu.SemaphoreType.DMA((2,)),
        pltpu.SemaphoreType.DMA((N_TB,)),
        pltpu.VMEM((D_MODEL, IN_COLS), BF16),
        pltpu.VMEM((LRU_HEADS, LRU_HEAD_DIM, 2 * LRU_HEAD_DIM), BF16),
        pltpu.VMEM((POOL_GROUPS, POOL_GROUP_DIM, POOL_GROUP_DIM), BF16),
        pltpu.VMEM((LRU_WIDTH, D_MODEL), BF16),
        pltpu.VMEM((POOL_WIDTH, D_MODEL), BF16),
        pltpu.VMEM((D_MODEL, D_MODEL), BF16),
        pltpu.VMEM((D_MODEL, D_MODEL), BF16),
        pltpu.VMEM((P_DIM, D_MODEL), BF16),
        pltpu.VMEM((2, ROWS, D_MODEL), BF16),
        pltpu.VMEM((ROWS, D_MODEL), BF16),
        pltpu.VMEM((ROWS, P_DIM), BF16),
        pltpu.VMEM((N_CHUNKS, ROWS + CONV_TAIL, CW), F32),
        pltpu.VMEM((ROWS + POOL_TAIL, POOL_WIDTH), F32),
        pltpu.VMEM((BATCH, LRU_WIDTH), F32),
        pltpu.VMEM((N_TB, ROWS, CW), F32),
        pltpu.VMEM((2, ROWS, CW), BF16),
        pltpu.VMEM((ROWS, LRU_WIDTH), BF16),
        pltpu.VMEM((ROWS, POOL_WIDTH), BF16),
        pltpu.VMEM((POOL_GROUPS, ROWS, POOL_GROUP_DIM), BF16),
        pltpu.VMEM((POOL_GROUPS, ROWS, POOL_GROUP_DIM), F32),
        pltpu.VMEM((ROWS, D_MODEL), BF16),
        pltpu.VMEM((ROWS, D_MODEL), F32),
    ]
    return pl.pallas_call(
        _block_kernel,
        grid=(SEQ // TT,),
        in_specs=in_specs,
        out_specs=pl.BlockSpec(memory_space=pl.ANY),
        out_shape=jax.ShapeDtypeStruct((BATCH, SEQ, D_MODEL), F32),
        scratch_shapes=scratch,
        compiler_params=pltpu.CompilerParams(
            dimension_semantics=("arbitrary",),
            vmem_limit_bytes=VMEM_LIMIT_BYTES,
        ),
        name="rglru_pool_block",
    )(*operands)
```

```python
import jax
import jax.numpy as jnp
from jax import lax
from jax.experimental import pallas as pl
from jax.experimental.pallas import tpu as pltpu

D_MODEL = 1024
BATCH = 16
SEQ = 2048
P_DIM = 256
LRU_WIDTH = 1024
LRU_HEADS = 8
LRU_HEAD_DIM = 128
CONV_WIDTH = 4
LRU_C = 8.0
POOL_WIDTH = 512
POOL_WINDOWS = (2, 4, 8, 16)
POOL_GROUPS = 4
POOL_GROUP_DIM = 128
MAX_WIN = 16
EPS = 1e-6

OFF_XA = 0
OFF_GA = LRU_WIDTH
OFF_XB = 2 * LRU_WIDTH
OFF_GB = OFF_XB + POOL_WIDTH
OFF_MA = OFF_GB + POOL_WIDTH
OFF_MB = OFF_MA + D_MODEL
IN_COLS = OFF_MB + D_MODEL

TT = 32
ROWS = TT * BATCH
CW = 256
N_CHUNKS = D_MODEL // CW
HEADS_PER_CHUNK = CW // LRU_HEAD_DIM
RT = 64
NORM_RT = 32
CONV_TAIL = (CONV_WIDTH - 1) * BATCH
POOL_TAIL = MAX_WIN * BATCH
X_SLOTS = 3
VMEM_LIMIT_BYTES = 58 * 1024 * 1024

TB_GA = 0
TB_XC, TB_G0, TB_G1, TB_A, TB_U = 4, 6, 8, 10, 12
TB_MA, TB_MB = 14, 18
TB_GB = 22
N_TB = 24
TB_PA, TB_PB, TB_PE, TB_GATE = 0, 2, 6, 10

F32 = jnp.float32
BF16 = jnp.bfloat16
TINY = 1e-30


def _tanh_p1(vh):
    return jnp.tanh(vh) + 1.0


def _dot(a, b):
    return jnp.dot(a, b, preferred_element_type=F32)


def _block_kernel(x_hbm, p_hbm, ng_ref, win_hbm, cw_ref, cb_ref, wa_hbm, ba_ref, wx_hbm, bx_ref,
                  lam_ref, pw_hbm, ps_ref, plru_hbm, ppool_hbm, wout_hbm, pg_ref, wpg_hbm,
                  wpe_hbm, fg_ref,
                  o_hbm,
                  xbuf, pbuf, obuf, sem_x, sem_p, sem_o, sem_w,
                  win_ref, wax_ref, pw_ref, plru_ref, ppool_ref, wout_ref, wpg_ref, wpe_ref,
                  hb_in, hb2, pbf, xa_ext, xb_ext, hst, tb, xcb, ya, yb, dbuf, ybuf, mbuf, x1):
    ti = pl.program_id(0)
    n_steps = pl.num_programs(0)
    slot = ti % 2
    nxt = jnp.minimum(ti + 1, n_steps - 1)
    nxt2 = jnp.minimum(ti + 2, n_steps - 1)
    nslot = 1 - slot
    xs, xs1, xs2 = ti % X_SLOTS, (ti + 1) % X_SLOTS, (ti + 2) % X_SLOTS

    def x_copy(step, sl, b):
        return pltpu.make_async_copy(x_hbm.at[b, pl.ds(step * TT, TT), :], xbuf.at[sl, :, b, :],
                                     sem_x.at[sl])

    def p_copy(step, sl, b):
        return pltpu.make_async_copy(p_hbm.at[0, b, pl.ds(step * TT, TT), :], pbuf.at[sl, :, b, :],
                                     sem_p.at[sl])

    def o_copy(step, sl, b):
        return pltpu.make_async_copy(obuf.at[sl, :, b, :], o_hbm.at[b, pl.ds(step * TT, TT), :],
                                     sem_o.at[sl])

    def start_copies(copy, step, sl):
        for b in range(BATCH):
            copy(step, sl, b).start()

    def wait_slot(buf, sem, sl):
        pltpu.make_async_copy(buf.at[sl], buf.at[sl], sem.at[sl]).wait()

    def x_rows(sl, r0, nrows, c0=0, ncols=D_MODEL):
        t0, nt = r0 // BATCH, nrows // BATCH
        return xbuf[sl, pl.ds(t0, nt), :, c0:c0 + ncols].reshape(nrows, ncols)

    def rmsnorm(xv, g_ref, cast):
        ms = jnp.mean(xv * xv, axis=-1, keepdims=True)
        return ((xv * lax.rsqrt(ms + EPS)) * g_ref[...]).astype(cast)

    def win(off, c):
        return win_ref[:, off + c * CW:off + (c + 1) * CW]

    def norm_in(xsl, sl):
        for r0 in range(0, ROWS, NORM_RT):
            hb_in[sl, pl.ds(r0, NORM_RT), :] = rmsnorm(x_rows(xsl, r0, NORM_RT), ng_ref, BF16)

    def dots_xa_ga(sl, c):
        xa_ext[c, CONV_TAIL:CONV_TAIL + ROWS, :] = _dot(hb_in[sl], win(OFF_XA, c))
        tb[TB_GA + c] = _dot(hb_in[sl], win(OFF_GA, c))

    def weight_tiles():
        tiles = []

        def add(src, dst, nrows, ncols, scale):
            for r0 in range(0, nrows, ROWS):
                nr = min(ROWS, nrows - r0)
                for c0 in range(0, ncols, CW):
                    tiles.append((src.at[0, pl.ds(r0, nr), pl.ds(c0, CW)], nr, CW,
                                  dst.at[pl.ds(r0, nr), pl.ds(c0, CW)],
                                  scale(c0) if callable(scale) else scale))

        halved_in = lambda c0: 0.5 if (OFF_GA <= c0 < OFF_XB or c0 >= OFF_GB) else 1.0
        add(win_hbm, win_ref, D_MODEL, IN_COLS, halved_in)
        add(plru_hbm, plru_ref, LRU_WIDTH, D_MODEL, 0.5)
        add(ppool_hbm, ppool_ref, POOL_WIDTH, D_MODEL, 0.5)
        add(wout_hbm, wout_ref, D_MODEL, D_MODEL, 1.0)
        add(wpg_hbm, wpg_ref, D_MODEL, D_MODEL, 0.5)
        add(wpe_hbm, wpe_ref, P_DIM, D_MODEL, 0.5)
        for h in range(LRU_HEADS):
            for part, src in enumerate((wa_hbm, wx_hbm)):
                tiles.append((src.at[0, h], LRU_HEAD_DIM, LRU_HEAD_DIM,
                              wax_ref.at[h, :, pl.ds(part * LRU_HEAD_DIM, LRU_HEAD_DIM)], 0.5))
        for g in range(POOL_GROUPS):
            tiles.append((pw_hbm.at[0, g], POOL_GROUP_DIM, POOL_GROUP_DIM, pw_ref.at[g], 1.0))
        return tiles

    def load_weights():
        tiles = weight_tiles()
        for w0 in range(0, len(tiles), N_TB):
            wave = tiles[w0:w0 + N_TB]
            copies = [pltpu.make_async_copy(src, tb.at[k, pl.ds(0, nr), pl.ds(0, nc)], sem_w.at[k])
                      for k, (src, nr, nc, _, _) in enumerate(wave)]
            for cp in copies:
                cp.start()
            for k, (cp, (_, nr, nc, dst, scale)) in enumerate(zip(copies, wave)):
                cp.wait()
                v = tb[k, 0:nr, 0:nc]
                dst[...] = (v if scale == 1.0 else v * scale).astype(BF16)

    @pl.when(ti == 0)
    def _():
        start_copies(x_copy, 0, 0)
        start_copies(x_copy, 1, 1)
        start_copies(p_copy, 0, 0)
        start_copies(p_copy, 1, 1)
        load_weights()
        xa_ext[:, 0:CONV_TAIL, :] = jnp.zeros((N_CHUNKS, CONV_TAIL, CW), F32)
        xb_ext[0:POOL_TAIL, :] = jnp.zeros((POOL_TAIL, POOL_WIDTH), F32)
        hst[...] = jnp.zeros((BATCH, LRU_WIDTH), F32)
        wait_slot(xbuf, sem_x, 0)
        norm_in(0, 0)
        dots_xa_ga(0, 0)
        dots_xa_ga(0, 1)

    @pl.when(ti >= 1)
    def _():
        start_copies(o_copy, ti - 1, nslot)

    @pl.when(ti >= 2)
    def _():
        wait_slot(obuf, sem_o, slot)

    wait_slot(pbuf, sem_p, xs)
    wait_slot(xbuf, sem_x, xs1)

    hb = hb_in.at[slot]
    for r0 in range(0, ROWS, RT):
        pv = pbuf[xs, pl.ds(r0 // BATCH, RT // BATCH), :, :].reshape(RT, P_DIM)
        pbf[pl.ds(r0, RT), :] = pv.astype(BF16)

    lam = lam_ref[...]
    half_c = (-0.5 * LRU_C) * (jnp.maximum(-lam, 0.0) + jnp.log1p(jnp.exp(-jnp.abs(lam))))

    def conv(c):
        q, c0 = c % 2, c * CW
        for r0 in range(0, ROWS, RT):
            rows = pl.ds(r0, RT)
            acc = cb_ref[:, c0:c0 + CW]
            for k in range(CONV_WIDTH):
                acc = acc + xa_ext[c, r0 + k * BATCH:r0 + k * BATCH + RT, :] * cw_ref[k:k + 1, c0:c0 + CW]
            tb[TB_XC + q, rows, :] = acc
            xcb[q, rows, :] = acc.astype(BF16)
        xa_ext[c, 0:CONV_TAIL, :] = xa_ext[c, ROWS:ROWS + CONV_TAIL, :]

    def dots_gates(c):
        q = c % 2
        for hh, tg in enumerate((TB_G0, TB_G1)):
            l0 = hh * LRU_HEAD_DIM
            tb[tg + q] = _dot(xcb[q, :, l0:l0 + LRU_HEAD_DIM], wax_ref[c * HEADS_PER_CHUNK + hh])

    def gates(c):
        q = c % 2
        for hh, tg in enumerate((TB_G0, TB_G1)):
            l0 = hh * LRU_HEAD_DIM
            ch = slice(c * CW + l0, c * CW + l0 + LRU_HEAD_DIM)
            lanes = slice(l0, l0 + LRU_HEAD_DIM)
            for r0 in range(0, ROWS, RT):
                rows = pl.ds(r0, RT)
                r_tanh = jnp.tanh(tb[tg + q, rows, 0:LRU_HEAD_DIM] + 0.5 * ba_ref[:, ch])
                i_gate = 0.5 * jnp.tanh(tb[tg + q, rows, LRU_HEAD_DIM:2 * LRU_HEAD_DIM] + 0.5 * bx_ref[:, ch]) + 0.5
                a = jnp.exp(half_c[:, ch] * r_tanh + half_c[:, ch])
                y = 1.0 - a * a
                mult = y * lax.rsqrt(jnp.maximum(y, TINY))
                tb[TB_A + q, rows, lanes] = a
                tb[TB_U + q, rows, lanes] = mult * (i_gate * tb[TB_XC + q, rows, lanes])

    def scan(c):
        q, c0 = c % 2, c * CW
        h = hst[:, c0:c0 + CW]
        for t in range(TT):
            rows = pl.ds(t * BATCH, BATCH)
            h = tb[TB_A + q, rows, :] * h + tb[TB_U + q, rows, :]
            ga = tb[TB_GA + c, rows, :]
            ya[rows, c0:c0 + CW] = (h * (_tanh_p1(ga) * ga)).astype(BF16)
        hst[:, c0:c0 + CW] = h

    def dots_pool_in():
        xb_ext[POOL_TAIL:POOL_TAIL + ROWS, :] = _dot(hb[...], win_ref[:, OFF_XB:OFF_XB + POOL_WIDTH])
        for half in range(POOL_WIDTH // CW):
            tb[TB_GB + half] = _dot(hb[...], win(OFF_GB, half))

    def pool_windows(g):
        k = POOL_WINDOWS[g]
        lanes = slice(g * POOL_GROUP_DIM, (g + 1) * POOL_GROUP_DIM)
        for r0 in range(0, ROWS, RT):
            cur = xb_ext[POOL_TAIL + r0:POOL_TAIL + r0 + RT, lanes]
            s = cur
            for j in range(1, k):
                s = s + xb_ext[POOL_TAIL + r0 - j * BATCH:POOL_TAIL + r0 - j * BATCH + RT, lanes]
            row = lax.broadcasted_iota(jnp.int32, (RT, POOL_GROUP_DIM), 0) + r0
            pos = ti * TT + row // BATCH
            cnt = jnp.minimum(pos + 1, k).astype(F32)
            dbuf[g, r0:r0 + RT, :] = (s / cnt - cur).astype(BF16)

    def dot_pool(g):
        ybuf[g] = _dot(dbuf[g], pw_ref[g])

    def pool_out(g):
        lanes = slice(g * POOL_GROUP_DIM, (g + 1) * POOL_GROUP_DIM)
        half, gg = divmod(g, CW // POOL_GROUP_DIM)
        for r0 in range(0, ROWS, RT):
            rows = pl.ds(r0, RT)
            gb = tb[TB_GB + half, rows, gg * POOL_GROUP_DIM:(gg + 1) * POOL_GROUP_DIM]
            y = ybuf[g, rows, :] * ps_ref[:, lanes]
            yb[rows, lanes] = (y * (_tanh_p1(gb) * gb)).astype(BF16)

    def dots_merge_logits(c):
        tb[TB_MA + c] = _dot(hb[...], win(OFF_MA, c))
        tb[TB_MB + c] = _dot(hb[...], win(OFF_MB, c))

    for c in range(N_CHUNKS):
        conv(c)
        dots_gates(c)
        if c + 2 < N_CHUNKS:
            dots_xa_ga(slot, c + 2)
        if c == 0:
            dots_pool_in()
        else:
            dots_merge_logits(c - 1)
        gates(c)
        scan(c)
        pool_windows(c)
        dot_pool(c)
    dots_merge_logits(N_CHUNKS - 1)
    xb_ext[0:POOL_TAIL, :] = xb_ext[ROWS:ROWS + POOL_TAIL, :]
    for g in range(POOL_GROUPS):
        pool_out(g)

    def dots_proj(c):
        c0 = c * CW
        tb[TB_PA + c % 2] = _dot(ya[...], plru_ref[:, c0:c0 + CW])
        tb[TB_PB + c % 2] = _dot(yb[...], ppool_ref[:, c0:c0 + CW])

    dots_proj(0)
    for c in range(N_CHUNKS):
        q, c0 = c % 2, c * CW
        if c + 1 < N_CHUNKS:
            dots_proj(c + 1)
        for r0 in range(0, ROWS, RT):
            rows = pl.ds(r0, RT)
            m = (_tanh_p1(tb[TB_MA + c, rows, :]) * tb[TB_PA + q, rows, :]
                 + _tanh_p1(tb[TB_MB + c, rows, :]) * tb[TB_PB + q, rows, :])
            mbuf[rows, c0:c0 + CW] = m.astype(BF16)

    for c in range(N_CHUNKS):
        c0 = c * CW
        x1[:, c0:c0 + CW] = x_rows(xs, 0, ROWS, c0, CW) + _dot(mbuf[...], wout_ref[:, c0:c0 + CW])
    norm_in(xs1, nslot)
    start_copies(x_copy, nxt2, xs2)
    start_copies(p_copy, nxt2, xs2)

    for c in range(N_CHUNKS):
        tb[TB_PE + c] = _dot(pbf[...], wpe_ref[:, c * CW:(c + 1) * CW])
    for r0 in range(0, ROWS, NORM_RT):
        hb2[pl.ds(r0, NORM_RT), :] = rmsnorm(x1[pl.ds(r0, NORM_RT), :], pg_ref, BF16)

    def dot_gate(c):
        tb[TB_GATE + c] = _dot(hb2[...], wpg_ref[:, c * CW:(c + 1) * CW])

    dot_gate(0)
    dot_gate(1)
    for c in range(N_CHUNKS):
        q, c0 = c % 2, c * CW
        if c + 2 < N_CHUNKS:
            dot_gate(c + 2)
        else:
            dots_xa_ga(nslot, c + 2 - N_CHUNKS)
        for r0 in range(0, ROWS, RT):
            rows = pl.ds(r0, RT)
            x1[rows, c0:c0 + CW] = (x1[rows, c0:c0 + CW]
                                    + _tanh_p1(tb[TB_GATE + c, rows, :]) * tb[TB_PE + c, rows, :])

    for r0 in range(0, ROWS, NORM_RT):
        ov = rmsnorm(x1[pl.ds(r0, NORM_RT), :], fg_ref, F32)
        obuf[slot, pl.ds(r0 // BATCH, NORM_RT // BATCH), :, :] = ov.reshape(NORM_RT // BATCH, BATCH, D_MODEL)

    @pl.when(ti == n_steps - 1)
    def _():
        start_copies(o_copy, ti, slot)
        wait_slot(xbuf, sem_x, xs2)
        wait_slot(pbuf, sem_p, xs1)
        wait_slot(pbuf, sem_p, xs2)
        wait_slot(obuf, sem_o, nslot)
        wait_slot(obuf, sem_o, slot)


def _const_spec(shape):
    zeros = (0,) * len(shape)
    return pl.BlockSpec(shape, lambda i: zeros, pipeline_mode=pl.Buffered(1))


@jax.jit
def kernel(x, p, norm_g, w_in, conv_w, conv_b, lru_w_a, lru_b_a, lru_w_x, lru_b_x, lru_lambda,
           pool_w, pool_scale, w_proj_lru, w_proj_pool, w_out, ple_norm_g, w_ple_gate,
           w_ple_proj, final_g):
    assert x.shape == (BATCH, SEQ, D_MODEL) and p.shape == (1, BATCH, SEQ, P_DIM)
    row = lambda v: v.reshape(1, -1)
    operands = [
        x, p, row(norm_g[0]), w_in, conv_w[0], row(conv_b[0]), lru_w_a, row(lru_b_a[0]), lru_w_x,
        row(lru_b_x[0]), row(lru_lambda[0]), pool_w, row(pool_scale[0]), w_proj_lru, w_proj_pool,
        w_out, row(ple_norm_g[0]), w_ple_gate, w_ple_proj, row(final_g),
    ]
    in_specs = [pl.BlockSpec(memory_space=pl.ANY) if v.ndim > 2 else _const_spec(v.shape)
                for v in operands]
    scratch = [
        pltpu.VMEM((X_SLOTS, TT, BATCH, D_MODEL), F32),
        pltpu.VMEM((X_SLOTS, TT, BATCH, P_DIM), F32),
        pltpu.VMEM((2, TT, BATCH, D_MODEL), F32),
        pltpu.SemaphoreType.DMA((X_SLOTS,)),
        pltpu.SemaphoreType.DMA((X_SLOTS,)),
        pltpu.SemaphoreType.DMA((2,)),
        pltpu.SemaphoreType.DMA((N_TB,)),
        pltpu.VMEM((D_MODEL, IN_COLS), BF16),
        pltpu.VMEM((LRU_HEADS, LRU_HEAD_DIM, 2 * LRU_HEAD_DIM), BF16),
        pltpu.VMEM((POOL_GROUPS, POOL_GROUP_DIM, POOL_GROUP_DIM), BF16),
        pltpu.VMEM((LRU_WIDTH, D_MODEL), BF16),
        pltpu.VMEM((POOL_WIDTH, D_MODEL), BF16),
        pltpu.VMEM((D_MODEL, D_MODEL), BF16),
        pltpu.VMEM((D_MODEL, D_MODEL), BF16),
        pltpu.VMEM((P_DIM, D_MODEL), BF16),
        pltpu.VMEM((2, ROWS, D_MODEL), BF16),
        pltpu.VMEM((ROWS, D_MODEL), BF16),
        pltpu.VMEM((ROWS, P_DIM), BF16),
        pltpu.VMEM((N_CHUNKS, ROWS + CONV_TAIL, CW), F32),
        pltpu.VMEM((ROWS + POOL_TAIL, POOL_WIDTH), F32),
        pltpu.VMEM((BATCH, LRU_WIDTH), F32),
        pltpu.VMEM((N_TB, ROWS, CW), F32),
        pltpu.VMEM((2, ROWS, CW), BF16),
        pltpu.VMEM((ROWS, LRU_WIDTH), BF16),
        pltpu.VMEM((ROWS, POOL_WIDTH), BF16),
        pltpu.VMEM((POOL_GROUPS, ROWS, POOL_GROUP_DIM), BF16),
        pltpu.VMEM((POOL_GROUPS, ROWS, POOL_GROUP_DIM), F32),
        pltpu.VMEM((ROWS, D_MODEL), BF16),
        pltpu.VMEM((ROWS, D_MODEL), F32),
    ]
    return pl.pallas_call(
        _block_kernel,
        grid=(SEQ // TT,),
        in_specs=in_specs,
        out_specs=pl.BlockSpec(memory_space=pl.ANY),
        out_shape=jax.ShapeDtypeStruct((BATCH, SEQ, D_MODEL), F32),
        scratch_shapes=scratch,
        compiler_params=pltpu.CompilerParams(
            dimension_semantics=("arbitrary",),
            vmem_limit_bytes=VMEM_LIMIT_BYTES,
        ),
        name="rglru_pool_block",
    )(*operands)
```

```python
import jax
import jax.numpy as jnp
from jax import lax
from jax.experimental import pallas as pl
from jax.experimental.pallas import tpu as pltpu

D_MODEL = 1024
BATCH = 16
SEQ = 2048
P_DIM = 256
LRU_WIDTH = 1024
LRU_HEADS = 8
LRU_HEAD_DIM = 128
CONV_WIDTH = 4
LRU_C = 8.0
POOL_WIDTH = 512
POOL_WINDOWS = (2, 4, 8, 16)
POOL_GROUPS = 4
POOL_GROUP_DIM = 128
MAX_WIN = 16
EPS = 1e-6

OFF_XA = 0
OFF_GA = LRU_WIDTH
OFF_XB = 2 * LRU_WIDTH
OFF_GB = OFF_XB + POOL_WIDTH
OFF_MA = OFF_GB + POOL_WIDTH
OFF_MB = OFF_MA + D_MODEL
IN_COLS = OFF_MB + D_MODEL

TT = 32
ROWS = TT * BATCH
CW = 256
N_CHUNKS = D_MODEL // CW
HEADS_PER_CHUNK = CW // LRU_HEAD_DIM
RT = 64
NORM_RT = 32
CONV_TAIL = (CONV_WIDTH - 1) * BATCH
POOL_TAIL = MAX_WIN * BATCH
X_SLOTS = 3
VMEM_LIMIT_BYTES = 58 * 1024 * 1024

TB_GA = 0
TB_XC, TB_G0, TB_G1, TB_A, TB_U = 4, 6, 8, 10, 12
TB_MA, TB_MB = 14, 18
TB_GB = 22
N_TB = 24
TB_PA, TB_PB, TB_PE, TB_GATE = 0, 2, 6, 10

F32 = jnp.float32
BF16 = jnp.bfloat16
TINY = 1e-30


def _tanh_p1(vh):
    return jnp.tanh(vh) + 1.0


def _dot(a, b):
    return jnp.dot(a, b, preferred_element_type=F32)


def _block_kernel(x_hbm, p_hbm, ng_ref, win_hbm, cw_ref, cb_ref, wa_hbm, ba_ref, wx_hbm, bx_ref,
                  lam_ref, pw_hbm, ps_ref, plru_hbm, ppool_hbm, wout_hbm, pg_ref, wpg_hbm,
                  wpe_hbm, fg_ref,
                  o_hbm,
                  xbuf, pbuf, obuf, sem_x, sem_p, sem_o, sem_w,
                  win_ref, wax_ref, pw_ref, plru_ref, ppool_ref, wout_ref, wpg_ref, wpe_ref,
                  hb_in, hb2, pbf, xa_ext, xb_ext, hst, tb, xcb, ya, yb, dbuf, ybuf, mbuf, x1):
    ti = pl.program_id(0)
    n_steps = pl.num_programs(0)
    slot = ti % 2
    nxt = jnp.minimum(ti + 1, n_steps - 1)
    nxt2 = jnp.minimum(ti + 2, n_steps - 1)
    nslot = 1 - slot
    xs, xs1, xs2 = ti % X_SLOTS, (ti + 1) % X_SLOTS, (ti + 2) % X_SLOTS

    def x_copy(step, sl, b):
        return pltpu.make_async_copy(x_hbm.at[b, pl.ds(step * TT, TT), :], xbuf.at[sl, :, b, :],
                                     sem_x.at[sl])

    def p_copy(step, sl, b):
        return pltpu.make_async_copy(p_hbm.at[0, b, pl.ds(step * TT, TT), :], pbuf.at[sl, :, b, :],
                                     sem_p.at[sl])

    def o_copy(step, sl, b):
        return pltpu.make_async_copy(obuf.at[sl, :, b, :], o_hbm.at[b, pl.ds(step * TT, TT), :],
                                     sem_o.at[sl])

    def start_copies(copy, step, sl):
        for b in range(BATCH):
            copy(step, sl, b).start()

    def wait_slot(buf, sem, sl):
        pltpu.make_async_copy(buf.at[sl], buf.at[sl], sem.at[sl]).wait()

    def x_rows(sl, r0, nrows, c0=0, ncols=D_MODEL):
        t0, nt = r0 // BATCH, nrows // BATCH
        return xbuf[sl, pl.ds(t0, nt), :, c0:c0 + ncols].reshape(nrows, ncols)

    def rmsnorm(xv, g_ref, cast):
        ms = jnp.mean(xv * xv, axis=-1, keepdims=True)
        return ((xv * lax.rsqrt(ms + EPS)) * g_ref[...]).astype(cast)

    def win(off, c):
        return win_ref[:, off + c * CW:off + (c + 1) * CW]

    def norm_in(xsl, sl):
        for r0 in range(0, ROWS, NORM_RT):
            hb_in[sl, pl.ds(r0, NORM_RT), :] = rmsnorm(x_rows(xsl, r0, NORM_RT), ng_ref, BF16)

    def dots_xa_ga(sl, c):
        xa_ext[c, CONV_TAIL:CONV_TAIL + ROWS, :] = _dot(hb_in[sl], win(OFF_XA, c))
        tb[TB_GA + c] = _dot(hb_in[sl], win(OFF_GA, c))

    def weight_tiles():
        tiles = []

        def add(src, dst, nrows, ncols, scale):
            for r0 in range(0, nrows, ROWS):
                nr = min(ROWS, nrows - r0)
                for c0 in range(0, ncols, CW):
                    tiles.append((src.at[0, pl.ds(r0, nr), pl.ds(c0, CW)], nr, CW,
                                  dst.at[pl.ds(r0, nr), pl.ds(c0, CW)],
                                  scale(c0) if callable(scale) else scale))

        halved_in = lambda c0: 0.5 if (OFF_GA <= c0 < OFF_XB or c0 >= OFF_GB) else 1.0
        add(win_hbm, win_ref, D_MODEL, IN_COLS, halved_in)
        add(plru_hbm, plru_ref, LRU_WIDTH, D_MODEL, 0.5)
        add(ppool_hbm, ppool_ref, POOL_WIDTH, D_MODEL, 0.5)
        add(wout_hbm, wout_ref, D_MODEL, D_MODEL, 1.0)
        add(wpg_hbm, wpg_ref, D_MODEL, D_MODEL, 0.5)
        add(wpe_hbm, wpe_ref, P_DIM, D_MODEL, 0.5)
        for h in range(LRU_HEADS):
            for part, src in enumerate((wa_hbm, wx_hbm)):
                tiles.append((src.at[0, h], LRU_HEAD_DIM, LRU_HEAD_DIM,
                              wax_ref.at[h, :, pl.ds(part * LRU_HEAD_DIM, LRU_HEAD_DIM)], 0.5))
        for g in range(POOL_GROUPS):
            tiles.append((pw_hbm.at[0, g], POOL_GROUP_DIM, POOL_GROUP_DIM, pw_ref.at[g], 1.0))
        return tiles

    def load_weights():
        tiles = weight_tiles()
        for w0 in range(0, len(tiles), N_TB):
            wave = tiles[w0:w0 + N_TB]
            copies = [pltpu.make_async_copy(src, tb.at[k, pl.ds(0, nr), pl.ds(0, nc)], sem_w.at[k])
                      for k, (src, nr, nc, _, _) in enumerate(wave)]
            for cp in copies:
                cp.start()
            for k, (cp, (_, nr, nc, dst, scale)) in enumerate(zip(copies, wave)):
                cp.wait()
                v = tb[k, 0:nr, 0:nc]
                dst[...] = (v if scale == 1.0 else v * scale).astype(BF16)

    @pl.when(ti == 0)
    def _():
        start_copies(x_copy, 0, 0)
        start_copies(x_copy, 1, 1)
        start_copies(p_copy, 0, 0)
        start_copies(p_copy, 1, 1)
        load_weights()
        xa_ext[:, 0:CONV_TAIL, :] = jnp.zeros((N_CHUNKS, CONV_TAIL, CW), F32)
        xb_ext[0:POOL_TAIL, :] = jnp.zeros((POOL_TAIL, POOL_WIDTH), F32)
        hst[...] = jnp.zeros((BATCH, LRU_WIDTH), F32)
        wait_slot(xbuf, sem_x, 0)
        norm_in(0, 0)
        dots_xa_ga(0, 0)
        dots_xa_ga(0, 1)

    @pl.when(ti >= 1)
    def _():
        start_copies(o_copy, ti - 1, nslot)

    @pl.when(ti >= 2)
    def _():
        wait_slot(obuf, sem_o, slot)

    wait_slot(pbuf, sem_p, xs)
    wait_slot(xbuf, sem_x, xs1)

    hb = hb_in.at[slot]
    for r0 in range(0, ROWS, RT):
        pv = pbuf[xs, pl.ds(r0 // BATCH, RT // BATCH), :, :].reshape(RT, P_DIM)
        pbf[pl.ds(r0, RT), :] = pv.astype(BF16)

    lam = lam_ref[...]
    half_c = (-0.5 * LRU_C) * (jnp.maximum(-lam, 0.0) + jnp.log1p(jnp.exp(-jnp.abs(lam))))

    def conv(c):
        q, c0 = c % 2, c * CW
        for r0 in range(0, ROWS, RT):
            rows = pl.ds(r0, RT)
            acc = cb_ref[:, c0:c0 + CW]
            for k in range(CONV_WIDTH):
                acc = acc + xa_ext[c, r0 + k * BATCH:r0 + k * BATCH + RT, :] * cw_ref[k:k + 1, c0:c0 + CW]
            tb[TB_XC + q, rows, :] = acc
            xcb[q, rows, :] = acc.astype(BF16)
        xa_ext[c, 0:CONV_TAIL, :] = xa_ext[c, ROWS:ROWS + CONV_TAIL, :]

    def dots_gates(c):
        q = c % 2
        for hh, tg in enumerate((TB_G0, TB_G1)):
            l0 = hh * LRU_HEAD_DIM
            tb[tg + q] = _dot(xcb[q, :, l0:l0 + LRU_HEAD_DIM], wax_ref[c * HEADS_PER_CHUNK + hh])

    def gates(c):
        q = c % 2
        for hh, tg in enumerate((TB_G0, TB_G1)):
            l0 = hh * LRU_HEAD_DIM
            ch = slice(c * CW + l0, c * CW + l0 + LRU_HEAD_DIM)
            lanes = slice(l0, l0 + LRU_HEAD_DIM)
            for r0 in range(0, ROWS, RT):
                rows = pl.ds(r0, RT)
                r_tanh = jnp.tanh(tb[tg + q, rows, 0:LRU_HEAD_DIM] + 0.5 * ba_ref[:, ch])
                i_gate = 0.5 * jnp.tanh(tb[tg + q, rows, LRU_HEAD_DIM:2 * LRU_HEAD_DIM] + 0.5 * bx_ref[:, ch]) + 0.5
                a = jnp.exp(half_c[:, ch] * r_tanh + half_c[:, ch])
                y = 1.0 - a * a
                mult = y * lax.rsqrt(jnp.maximum(y, TINY))
                tb[TB_A + q, rows, lanes] = a
                tb[TB_U + q, rows, lanes] = mult * (i_gate * tb[TB_XC + q, rows, lanes])

    def scan(c):
        q, c0 = c % 2, c * CW
        h = hst[:, c0:c0 + CW]
        for t in range(TT):
            rows = pl.ds(t * BATCH, BATCH)
            h = tb[TB_A + q, rows, :] * h + tb[TB_U + q, rows, :]
            ga = tb[TB_GA + c, rows, :]
            ya[rows, c0:c0 + CW] = (h * (_tanh_p1(ga) * ga)).astype(BF16)
        hst[:, c0:c0 + CW] = h

    def dots_pool_in():
        xb_ext[POOL_TAIL:POOL_TAIL + ROWS, :] = _dot(hb[...], win_ref[:, OFF_XB:OFF_XB + POOL_WIDTH])
        for half in range(POOL_WIDTH // CW):
            tb[TB_GB + half] = _dot(hb[...], win(OFF_GB, half))

    def pool_windows(g):
        k = POOL_WINDOWS[g]
        lanes = slice(g * POOL_GROUP_DIM, (g + 1) * POOL_GROUP_DIM)
        for r0 in range(0, ROWS, RT):
            cur = xb_ext[POOL_TAIL + r0:POOL_TAIL + r0 + RT, lanes]
            s = cur
            for j in range(1, k):
                s = s + xb_ext[POOL_TAIL + r0 - j * BATCH:POOL_TAIL + r0 - j * BATCH + RT, lanes]
            row = lax.broadcasted_iota(jnp.int32, (RT, POOL_GROUP_DIM), 0) + r0
            pos = ti * TT + row // BATCH
            cnt = jnp.minimum(pos + 1, k).astype(F32)
            dbuf[g, r0:r0 + RT, :] = (s / cnt - cur).astype(BF16)

    def dot_pool(g):
        ybuf[g] = _dot(dbuf[g], pw_ref[g])

    def pool_out(g):
        lanes = slice(g * POOL_GROUP_DIM, (g + 1) * POOL_GROUP_DIM)
        half, gg = divmod(g, CW // POOL_GROUP_DIM)
        for r0 in range(0, ROWS, RT):
            rows = pl.ds(r0, RT)
            gb = tb[TB_GB + half, rows, gg * POOL_GROUP_DIM:(gg + 1) * POOL_GROUP_DIM]
            y = ybuf[g, rows, :] * ps_ref[:, lanes]
            yb[rows, lanes] = (y * (_tanh_p1(gb) * gb)).astype(BF16)

    def dots_merge_logits(c):
        tb[TB_MA + c] = _dot(hb[...], win(OFF_MA, c))
        tb[TB_MB + c] = _dot(hb[...], win(OFF_MB, c))

    for c in range(N_CHUNKS):
        conv(c)
        dots_gates(c)
        if c + 2 < N_CHUNKS:
            dots_xa_ga(slot, c + 2)
        if c == 0:
            dots_pool_in()
        else:
            dots_merge_logits(c - 1)
        gates(c)
        scan(c)
        pool_windows(c)
        dot_pool(c)
    dots_merge_logits(N_CHUNKS - 1)
    xb_ext[0:POOL_TAIL, :] = xb_ext[ROWS:ROWS + POOL_TAIL, :]
    for g in range(POOL_GROUPS):
        pool_out(g)

    def dots_proj(c):
        c0 = c * CW
        tb[TB_PA + c % 2] = _dot(ya[...], plru_ref[:, c0:c0 + CW])
        tb[TB_PB + c % 2] = _dot(yb[...], ppool_ref[:, c0:c0 + CW])

    dots_proj(0)
    for c in range(N_CHUNKS):
        q, c0 = c % 2, c * CW
        if c + 1 < N_CHUNKS:
            dots_proj(c + 1)
        for r0 in range(0, ROWS, RT):
            rows = pl.ds(r0, RT)
            m = (_tanh_p1(tb[TB_MA + c, rows, :]) * tb[TB_PA + q, rows, :]
                 + _tanh_p1(tb[TB_MB + c, rows, :]) * tb[TB_PB + q, rows, :])
            mbuf[rows, c0:c0 + CW] = m.astype(BF16)

    for c in range(N_CHUNKS):
        c0 = c * CW
        x1[:, c0:c0 + CW] = x_rows(xs, 0, ROWS, c0, CW) + _dot(mbuf[...], wout_ref[:, c0:c0 + CW])
    norm_in(xs1, nslot)
    start_copies(x_copy, nxt2, xs2)
    start_copies(p_copy, nxt2, xs2)

    for c in range(N_CHUNKS):
        tb[TB_PE + c] = _dot(pbf[...], wpe_ref[:, c * CW:(c + 1) * CW])
    for r0 in range(0, ROWS, NORM_RT):
        hb2[pl.ds(r0, NORM_RT), :] = rmsnorm(x1[pl.ds(r0, NORM_RT), :], pg_ref, BF16)

    for c in range(N_CHUNKS):
        c0 = c * CW
        gate = _dot(hb2[...], wpg_ref[:, c0:c0 + CW])
        x1[:, c0:c0 + CW] = x1[:, c0:c0 + CW] + _tanh_p1(gate) * tb[TB_PE + c]
    dots_xa_ga(nslot, 0)
    dots_xa_ga(nslot, 1)

    for r0 in range(0, ROWS, NORM_RT):
        ov = rmsnorm(x1[pl.ds(r0, NORM_RT), :], fg_ref, F32)
        obuf[slot, pl.ds(r0 // BATCH, NORM_RT // BATCH), :, :] = ov.reshape(NORM_RT // BATCH, BATCH, D_MODEL)

    @pl.when(ti == n_steps - 1)
    def _():
        start_copies(o_copy, ti, slot)
        wait_slot(xbuf, sem_x, xs2)
        wait_slot(pbuf, sem_p, xs1)
        wait_slot(pbuf, sem_p, xs2)
        wait_slot(obuf, sem_o, nslot)
        wait_slot(obuf, sem_o, slot)


def _const_spec(shape):
    zeros = (0,) * len(shape)
    return pl.BlockSpec(shape, lambda i: zeros, pipeline_mode=pl.Buffered(1))


@jax.jit
def kernel(x, p, norm_g, w_in, conv_w, conv_b, lru_w_a, lru_b_a, lru_w_x, lru_b_x, lru_lambda,
           pool_w, pool_scale, w_proj_lru, w_proj_pool, w_out, ple_norm_g, w_ple_gate,
           w_ple_proj, final_g):
    assert x.shape == (BATCH, SEQ, D_MODEL) and p.shape == (1, BATCH, SEQ, P_DIM)
    row = lambda v: v.reshape(1, -1)
    operands = [
        x, p, row(norm_g[0]), w_in, conv_w[0], row(conv_b[0]), lru_w_a, row(lru_b_a[0]), lru_w_x,
        row(lru_b_x[0]), row(lru_lambda[0]), pool_w, row(pool_scale[0]), w_proj_lru, w_proj_pool,
        w_out, row(ple_norm_g[0]), w_ple_gate, w_ple_proj, row(final_g),
    ]
    in_specs = [pl.BlockSpec(memory_space=pl.ANY) if v.ndim > 2 else _const_spec(v.shape)
                for v in operands]
    scratch = [
        pltpu.VMEM((X_SLOTS, TT, BATCH, D_MODEL), F32),
        pltpu.VMEM((X_SLOTS, TT, BATCH, P_DIM), F32),
        pltpu.VMEM((2, TT, BATCH, D_MODEL), F32),
        pltpu.SemaphoreType.DMA((X_SLOTS,)),
        pltpu.SemaphoreType.DMA((X_SLOTS,)),
        pltpu.SemaphoreType.DMA((2,)),
        pltpu.SemaphoreType.DMA((N_TB,)),
        pltpu.VMEM((D_MODEL, IN_COLS), BF16),
        pltpu.VMEM((LRU_HEADS, LRU_HEAD_DIM, 2 * LRU_HEAD_DIM), BF16),
        pltpu.VMEM((POOL_GROUPS, POOL_GROUP_DIM, POOL_GROUP_DIM), BF16),
        pltpu.VMEM((LRU_WIDTH, D_MODEL), BF16),
        pltpu.VMEM((POOL_WIDTH, D_MODEL), BF16),
        pltpu.VMEM((D_MODEL, D_MODEL), BF16),
        pltpu.VMEM((D_MODEL, D_MODEL), BF16),
        pltpu.VMEM((P_DIM, D_MODEL), BF16),
        pltpu.VMEM((2, ROWS, D_MODEL), BF16),
        pltpu.VMEM((ROWS, D_MODEL), BF16),
        pltpu.VMEM((ROWS, P_DIM), BF16),
        pltpu.VMEM((N_CHUNKS, ROWS + CONV_TAIL, CW), F32),
        pltpu.VMEM((ROWS + POOL_TAIL, POOL_WIDTH), F32),
        pltpu.VMEM((BATCH, LRU_WIDTH), F32),
        pltpu.VMEM((N_TB, ROWS, CW), F32),
        pltpu.VMEM((2, ROWS, CW), BF16),
        pltpu.VMEM((ROWS, LRU_WIDTH), BF16),
        pltpu.VMEM((ROWS, POOL_WIDTH), BF16),
        pltpu.VMEM((POOL_GROUPS, ROWS, POOL_GROUP_DIM), BF16),
        pltpu.VMEM((POOL_GROUPS, ROWS, POOL_GROUP_DIM), F32),
        pltpu.VMEM((ROWS, D_MODEL), BF16),
        pltpu.VMEM((ROWS, D_MODEL), F32),
    ]
    return pl.pallas_call(
        _block_kernel,
        grid=(SEQ // TT,),
        in_specs=in_specs,
        out_specs=pl.BlockSpec(memory_space=pl.ANY),
        out_shape=jax.ShapeDtypeStruct((BATCH, SEQ, D_MODEL), F32),
        scratch_shapes=scratch,
        compiler_params=pltpu.CompilerParams(
            dimension_semantics=("arbitrary",),
            vmem_limit_bytes=VMEM_LIMIT_BYTES,
        ),
        name="rglru_pool_block",
    )(*operands)
```

```python
import jax
import jax.numpy as jnp
from jax import lax
from jax.experimental import pallas as pl
from jax.experimental.pallas import tpu as pltpu

D_MODEL = 1024
BATCH = 16
SEQ = 2048
P_DIM = 256
LRU_WIDTH = 1024
LRU_HEADS = 8
LRU_HEAD_DIM = 128
CONV_WIDTH = 4
LRU_C = 8.0
POOL_WIDTH = 512
POOL_WINDOWS = (2, 4, 8, 16)
POOL_GROUPS = 4
POOL_GROUP_DIM = 128
MAX_WIN = 16
EPS = 1e-6

OFF_XA = 0
OFF_GA = LRU_WIDTH
OFF_XB = 2 * LRU_WIDTH
OFF_GB = OFF_XB + POOL_WIDTH
OFF_MA = OFF_GB + POOL_WIDTH
OFF_MB = OFF_MA + D_MODEL
IN_COLS = OFF_MB + D_MODEL

TT = 32
ROWS = TT * BATCH
CW = 256
N_CHUNKS = D_MODEL // CW
HEADS_PER_CHUNK = CW // LRU_HEAD_DIM
RT = 64
NORM_RT = 32
CONV_TAIL = (CONV_WIDTH - 1) * BATCH
POOL_TAIL = MAX_WIN * BATCH
X_SLOTS = 3
VMEM_LIMIT_BYTES = 58 * 1024 * 1024

TB_GA = 0
TB_XC, TB_G0, TB_G1, TB_A, TB_U = 4, 6, 8, 10, 12
TB_MA, TB_MB = 14, 18
TB_GB = 22
N_TB = 24
TB_PE = 6

F32 = jnp.float32
BF16 = jnp.bfloat16
TINY = 1e-30


def _tanh_p1(vh):
    return jnp.tanh(vh) + 1.0


def _dot(a, b):
    return jnp.dot(a, b, preferred_element_type=F32)


def _block_kernel(x_hbm, p_hbm, ng_ref, win_hbm, cw_ref, cb_ref, wa_hbm, ba_ref, wx_hbm, bx_ref,
                  lam_ref, pw_hbm, ps_ref, plru_hbm, ppool_hbm, wout_hbm, pg_ref, wpg_hbm,
                  wpe_hbm, fg_ref,
                  o_hbm,
                  xbuf, pbuf, obuf, sem_x, sem_p, sem_o, sem_w,
                  win_ref, wax_ref, pw_ref, plru_ref, ppool_ref, wout_ref, wpg_ref, wpe_ref,
                  hb_in, hb2, pbf, xa_ext, xb_ext, hst, tb, xcb, ya, yb, dbuf, ybuf, mbuf, x1):
    ti = pl.program_id(0)
    n_steps = pl.num_programs(0)
    slot = ti % 2
    nxt = jnp.minimum(ti + 1, n_steps - 1)
    nxt2 = jnp.minimum(ti + 2, n_steps - 1)
    nslot = 1 - slot
    xs, xs1, xs2 = ti % X_SLOTS, (ti + 1) % X_SLOTS, (ti + 2) % X_SLOTS

    def x_copy(step, sl, b):
        return pltpu.make_async_copy(x_hbm.at[b, pl.ds(step * TT, TT), :], xbuf.at[sl, :, b, :],
                                     sem_x.at[sl])

    def p_copy(step, sl, b):
        return pltpu.make_async_copy(p_hbm.at[0, b, pl.ds(step * TT, TT), :], pbuf.at[sl, :, b, :],
                                     sem_p.at[sl])

    def o_copy(step, sl, b):
        return pltpu.make_async_copy(obuf.at[sl, :, b, :], o_hbm.at[b, pl.ds(step * TT, TT), :],
                                     sem_o.at[sl])

    def start_copies(copy, step, sl):
        for b in range(BATCH):
            copy(step, sl, b).start()

    def wait_slot(buf, sem, sl):
        pltpu.make_async_copy(buf.at[sl], buf.at[sl], sem.at[sl]).wait()

    def x_rows(sl, r0, nrows, c0=0, ncols=D_MODEL):
        t0, nt = r0 // BATCH, nrows // BATCH
        return xbuf[sl, pl.ds(t0, nt), :, c0:c0 + ncols].reshape(nrows, ncols)

    def rmsnorm(xv, g_ref, cast):
        ms = jnp.mean(xv * xv, axis=-1, keepdims=True)
        return ((xv * lax.rsqrt(ms + EPS)) * g_ref[...]).astype(cast)

    def win(off, c):
        return win_ref[:, off + c * CW:off + (c + 1) * CW]

    def norm_in(xsl, sl):
        for r0 in range(0, ROWS, NORM_RT):
            hb_in[sl, pl.ds(r0, NORM_RT), :] = rmsnorm(x_rows(xsl, r0, NORM_RT), ng_ref, BF16)

    def dots_xa_ga(sl, c):
        xa_ext[c, CONV_TAIL:CONV_TAIL + ROWS, :] = _dot(hb_in[sl], win(OFF_XA, c))
        tb[TB_GA + c] = _dot(hb_in[sl], win(OFF_GA, c))

    def weight_tiles():
        tiles = []

        def add(src, dst, nrows, ncols, scale):
            for r0 in range(0, nrows, ROWS):
                nr = min(ROWS, nrows - r0)
                for c0 in range(0, ncols, CW):
                    tiles.append((src.at[0, pl.ds(r0, nr), pl.ds(c0, CW)], nr, CW,
                                  dst.at[pl.ds(r0, nr), pl.ds(c0, CW)],
                                  scale(c0) if callable(scale) else scale))

        halved_in = lambda c0: 0.5 if (OFF_GA <= c0 < OFF_XB or c0 >= OFF_GB) else 1.0
        add(win_hbm, win_ref, D_MODEL, IN_COLS, halved_in)
        add(plru_hbm, plru_ref, LRU_WIDTH, D_MODEL, 0.5)
        add(ppool_hbm, ppool_ref, POOL_WIDTH, D_MODEL, 0.5)
        add(wout_hbm, wout_ref, D_MODEL, D_MODEL, 1.0)
        add(wpg_hbm, wpg_ref, D_MODEL, D_MODEL, 0.5)
        add(wpe_hbm, wpe_ref, P_DIM, D_MODEL, 0.5)
        for h in range(LRU_HEADS):
            for part, src in enumerate((wa_hbm, wx_hbm)):
                tiles.append((src.at[0, h], LRU_HEAD_DIM, LRU_HEAD_DIM,
                              wax_ref.at[h, :, pl.ds(part * LRU_HEAD_DIM, LRU_HEAD_DIM)], 0.5))
        for g in range(POOL_GROUPS):
            tiles.append((pw_hbm.at[0, g], POOL_GROUP_DIM, POOL_GROUP_DIM, pw_ref.at[g], 1.0))
        return tiles

    def load_weights():
        tiles = weight_tiles()
        for w0 in range(0, len(tiles), N_TB):
            wave = tiles[w0:w0 + N_TB]
            copies = [pltpu.make_async_copy(src, tb.at[k, pl.ds(0, nr), pl.ds(0, nc)], sem_w.at[k])
                      for k, (src, nr, nc, _, _) in enumerate(wave)]
            for cp in copies:
                cp.start()
            for k, (cp, (_, nr, nc, dst, scale)) in enumerate(zip(copies, wave)):
                cp.wait()
                v = tb[k, 0:nr, 0:nc]
                dst[...] = (v if scale == 1.0 else v * scale).astype(BF16)

    @pl.when(ti == 0)
    def _():
        start_copies(x_copy, 0, 0)
        start_copies(x_copy, 1, 1)
        start_copies(p_copy, 0, 0)
        start_copies(p_copy, 1, 1)
        load_weights()
        xa_ext[:, 0:CONV_TAIL, :] = jnp.zeros((N_CHUNKS, CONV_TAIL, CW), F32)
        xb_ext[0:POOL_TAIL, :] = jnp.zeros((POOL_TAIL, POOL_WIDTH), F32)
        hst[...] = jnp.zeros((BATCH, LRU_WIDTH), F32)
        wait_slot(xbuf, sem_x, 0)
        norm_in(0, 0)
        dots_xa_ga(0, 0)
        dots_xa_ga(0, 1)

    @pl.when(ti >= 1)
    def _():
        start_copies(o_copy, ti - 1, nslot)

    @pl.when(ti >= 2)
    def _():
        wait_slot(obuf, sem_o, slot)

    wait_slot(pbuf, sem_p, xs)
    wait_slot(xbuf, sem_x, xs1)

    hb = hb_in.at[slot]
    for r0 in range(0, ROWS, RT):
        pv = pbuf[xs, pl.ds(r0 // BATCH, RT // BATCH), :, :].reshape(RT, P_DIM)
        pbf[pl.ds(r0, RT), :] = pv.astype(BF16)

    lam = lam_ref[...]
    half_c = (-0.5 * LRU_C) * (jnp.maximum(-lam, 0.0) + jnp.log1p(jnp.exp(-jnp.abs(lam))))

    def conv(c):
        q, c0 = c % 2, c * CW
        for r0 in range(0, ROWS, RT):
            rows = pl.ds(r0, RT)
            acc = cb_ref[:, c0:c0 + CW]
            for k in range(CONV_WIDTH):
                acc = acc + xa_ext[c, r0 + k * BATCH:r0 + k * BATCH + RT, :] * cw_ref[k:k + 1, c0:c0 + CW]
            tb[TB_XC + q, rows, :] = acc
            xcb[q, rows, :] = acc.astype(BF16)
        xa_ext[c, 0:CONV_TAIL, :] = xa_ext[c, ROWS:ROWS + CONV_TAIL, :]

    def dots_gates(c):
        q = c % 2
        for hh, tg in enumerate((TB_G0, TB_G1)):
            l0 = hh * LRU_HEAD_DIM
            tb[tg + q] = _dot(xcb[q, :, l0:l0 + LRU_HEAD_DIM], wax_ref[c * HEADS_PER_CHUNK + hh])

    def gates(c):
        q = c % 2
        for hh, tg in enumerate((TB_G0, TB_G1)):
            l0 = hh * LRU_HEAD_DIM
            ch = slice(c * CW + l0, c * CW + l0 + LRU_HEAD_DIM)
            lanes = slice(l0, l0 + LRU_HEAD_DIM)
            for r0 in range(0, ROWS, RT):
                rows = pl.ds(r0, RT)
                r_tanh = jnp.tanh(tb[tg + q, rows, 0:LRU_HEAD_DIM] + 0.5 * ba_ref[:, ch])
                i_gate = 0.5 * jnp.tanh(tb[tg + q, rows, LRU_HEAD_DIM:2 * LRU_HEAD_DIM] + 0.5 * bx_ref[:, ch]) + 0.5
                a = jnp.exp(half_c[:, ch] * r_tanh + half_c[:, ch])
                y = 1.0 - a * a
                mult = y * lax.rsqrt(jnp.maximum(y, TINY))
                tb[TB_A + q, rows, lanes] = a
                tb[TB_U + q, rows, lanes] = mult * (i_gate * tb[TB_XC + q, rows, lanes])

    def scan(c):
        q, c0 = c % 2, c * CW
        h = hst[:, c0:c0 + CW]
        for t in range(TT):
            rows = pl.ds(t * BATCH, BATCH)
            h = tb[TB_A + q, rows, :] * h + tb[TB_U + q, rows, :]
            ga = tb[TB_GA + c, rows, :]
            ya[rows, c0:c0 + CW] = (h * (_tanh_p1(ga) * ga)).astype(BF16)
        hst[:, c0:c0 + CW] = h

    def dots_pool_in():
        xb_ext[POOL_TAIL:POOL_TAIL + ROWS, :] = _dot(hb[...], win_ref[:, OFF_XB:OFF_XB + POOL_WIDTH])
        for half in range(POOL_WIDTH // CW):
            tb[TB_GB + half] = _dot(hb[...], win(OFF_GB, half))

    def pool_windows(g):
        k = POOL_WINDOWS[g]
        lanes = slice(g * POOL_GROUP_DIM, (g + 1) * POOL_GROUP_DIM)
        for r0 in range(0, ROWS, RT):
            cur = xb_ext[POOL_TAIL + r0:POOL_TAIL + r0 + RT, lanes]
            s = cur
            for j in range(1, k):
                s = s + xb_ext[POOL_TAIL + r0 - j * BATCH:POOL_TAIL + r0 - j * BATCH + RT, lanes]
            row = lax.broadcasted_iota(jnp.int32, (RT, POOL_GROUP_DIM), 0) + r0
            pos = ti * TT + row // BATCH
            cnt = jnp.minimum(pos + 1, k).astype(F32)
            dbuf[g, r0:r0 + RT, :] = (s / cnt - cur).astype(BF16)

    def dot_pool(g):
        ybuf[g] = _dot(dbuf[g], pw_ref[g])

    def pool_out(g):
        lanes = slice(g * POOL_GROUP_DIM, (g + 1) * POOL_GROUP_DIM)
        half, gg = divmod(g, CW // POOL_GROUP_DIM)
        for r0 in range(0, ROWS, RT):
            rows = pl.ds(r0, RT)
            gb = tb[TB_GB + half, rows, gg * POOL_GROUP_DIM:(gg + 1) * POOL_GROUP_DIM]
            y = ybuf[g, rows, :] * ps_ref[:, lanes]
            yb[rows, lanes] = (y * (_tanh_p1(gb) * gb)).astype(BF16)

    def dots_merge_logits(c):
        tb[TB_MA + c] = _dot(hb[...], win(OFF_MA, c))
        tb[TB_MB + c] = _dot(hb[...], win(OFF_MB, c))

    for c in range(N_CHUNKS):
        conv(c)
        dots_gates(c)
        if c + 2 < N_CHUNKS:
            dots_xa_ga(slot, c + 2)
        if c == 0:
            dots_pool_in()
        else:
            dots_merge_logits(c - 1)
        gates(c)
        scan(c)
        pool_windows(c)
        dot_pool(c)
    dots_merge_logits(N_CHUNKS - 1)
    xb_ext[0:POOL_TAIL, :] = xb_ext[ROWS:ROWS + POOL_TAIL, :]
    for g in range(POOL_GROUPS):
        pool_out(g)

    for c in range(N_CHUNKS):
        c0 = c * CW
        pa = _dot(ya[...], plru_ref[:, c0:c0 + CW])
        pb = _dot(yb[...], ppool_ref[:, c0:c0 + CW])
        mbuf[:, c0:c0 + CW] = (_tanh_p1(tb[TB_MA + c]) * pa + _tanh_p1(tb[TB_MB + c]) * pb).astype(BF16)

    for c in range(N_CHUNKS):
        c0 = c * CW
        x1[:, c0:c0 + CW] = x_rows(xs, 0, ROWS, c0, CW) + _dot(mbuf[...], wout_ref[:, c0:c0 + CW])
    norm_in(xs1, nslot)
    start_copies(x_copy, nxt2, xs2)
    start_copies(p_copy, nxt2, xs2)

    for c in range(N_CHUNKS):
        tb[TB_PE + c] = _dot(pbf[...], wpe_ref[:, c * CW:(c + 1) * CW])
    for r0 in range(0, ROWS, NORM_RT):
        hb2[pl.ds(r0, NORM_RT), :] = rmsnorm(x1[pl.ds(r0, NORM_RT), :], pg_ref, BF16)

    for c in range(N_CHUNKS):
        c0 = c * CW
        gate = _dot(hb2[...], wpg_ref[:, c0:c0 + CW])
        x1[:, c0:c0 + CW] = x1[:, c0:c0 + CW] + _tanh_p1(gate) * tb[TB_PE + c]
    dots_xa_ga(nslot, 0)
    dots_xa_ga(nslot, 1)

    for r0 in range(0, ROWS, NORM_RT):
        ov = rmsnorm(x1[pl.ds(r0, NORM_RT), :], fg_ref, F32)
        obuf[slot, pl.ds(r0 // BATCH, NORM_RT // BATCH), :, :] = ov.reshape(NORM_RT // BATCH, BATCH, D_MODEL)

    @pl.when(ti == n_steps - 1)
    def _():
        start_copies(o_copy, ti, slot)
        wait_slot(xbuf, sem_x, xs2)
        wait_slot(pbuf, sem_p, xs1)
        wait_slot(pbuf, sem_p, xs2)
        wait_slot(obuf, sem_o, nslot)
        wait_slot(obuf, sem_o, slot)


def _const_spec(shape):
    zeros = (0,) * len(shape)
    return pl.BlockSpec(shape, lambda i: zeros, pipeline_mode=pl.Buffered(1))


@jax.jit
def kernel(x, p, norm_g, w_in, conv_w, conv_b, lru_w_a, lru_b_a, lru_w_x, lru_b_x, lru_lambda,
           pool_w, pool_scale, w_proj_lru, w_proj_pool, w_out, ple_norm_g, w_ple_gate,
           w_ple_proj, final_g):
    assert x.shape == (BATCH, SEQ, D_MODEL) and p.shape == (1, BATCH, SEQ, P_DIM)
    row = lambda v: v.reshape(1, -1)
    operands = [
        x, p, row(norm_g[0]), w_in, conv_w[0], row(conv_b[0]), lru_w_a, row(lru_b_a[0]), lru_w_x,
        row(lru_b_x[0]), row(lru_lambda[0]), pool_w, row(pool_scale[0]), w_proj_lru, w_proj_pool,
        w_out, row(ple_norm_g[0]), w_ple_gate, w_ple_proj, row(final_g),
    ]
    in_specs = [pl.BlockSpec(memory_space=pl.ANY) if v.ndim > 2 else _const_spec(v.shape)
                for v in operands]
    scratch = [
        pltpu.VMEM((X_SLOTS, TT, BATCH, D_MODEL), F32),
        pltpu.VMEM((X_SLOTS, TT, BATCH, P_DIM), F32),
        pltpu.VMEM((2, TT, BATCH, D_MODEL), F32),
        pltpu.SemaphoreType.DMA((X_SLOTS,)),
        pltpu.SemaphoreType.DMA((X_SLOTS,)),
        pltpu.SemaphoreType.DMA((2,)),
        pltpu.SemaphoreType.DMA((N_TB,)),
        pltpu.VMEM((D_MODEL, IN_COLS), BF16),
        pltpu.VMEM((LRU_HEADS, LRU_HEAD_DIM, 2 * LRU_HEAD_DIM), BF16),
        pltpu.VMEM((POOL_GROUPS, POOL_GROUP_DIM, POOL_GROUP_DIM), BF16),
        pltpu.VMEM((LRU_WIDTH, D_MODEL), BF16),
        pltpu.VMEM((POOL_WIDTH, D_MODEL), BF16),
        pltpu.VMEM((D_MODEL, D_MODEL), BF16),
        pltpu.VMEM((D_MODEL, D_MODEL), BF16),
        pltpu.VMEM((P_DIM, D_MODEL), BF16),
        pltpu.VMEM((2, ROWS, D_MODEL), BF16),
        pltpu.VMEM((ROWS, D_MODEL), BF16),
        pltpu.VMEM((ROWS, P_DIM), BF16),
        pltpu.VMEM((N_CHUNKS, ROWS + CONV_TAIL, CW), F32),
        pltpu.VMEM((ROWS + POOL_TAIL, POOL_WIDTH), F32),
        pltpu.VMEM((BATCH, LRU_WIDTH), F32),
        pltpu.VMEM((N_TB, ROWS, CW), F32),
        pltpu.VMEM((2, ROWS, CW), BF16),
        pltpu.VMEM((ROWS, LRU_WIDTH), BF16),
        pltpu.VMEM((ROWS, POOL_WIDTH), BF16),
        pltpu.VMEM((POOL_GROUPS, ROWS, POOL_GROUP_DIM), BF16),
        pltpu.VMEM((POOL_GROUPS, ROWS, POOL_GROUP_DIM), F32),
        pltpu.VMEM((ROWS, D_MODEL), BF16),
        pltpu.VMEM((ROWS, D_MODEL), F32),
    ]
    return pl.pallas_call(
        _block_kernel,
        grid=(SEQ // TT,),
        in_specs=in_specs,
        out_specs=pl.BlockSpec(memory_space=pl.ANY),
        out_shape=jax.ShapeDtypeStruct((BATCH, SEQ, D_MODEL), F32),
        scratch_shapes=scratch,
        compiler_params=pltpu.CompilerParams(
            dimension_semantics=("arbitrary",),
            vmem_limit_bytes=VMEM_LIMIT_BYTES,
        ),
        name="rglru_pool_block",
    )(*operands)
```

```python
import math

import jax
import jax.numpy as jnp
from jax import lax
from jax.experimental import pallas as pl
from jax.experimental.pallas import tpu as pltpu

D_MODEL = 1024
BATCH = 16
SEQ = 2048
P_DIM = 256
LRU_WIDTH = 1024
LRU_HEADS = 8
LRU_HEAD_DIM = 128
CONV_WIDTH = 4
LRU_C = 8.0
POOL_WIDTH = 512
POOL_WINDOWS = (2, 4, 8, 16)
POOL_GROUPS = 4
POOL_GROUP_DIM = 128
MAX_WIN = 16
EPS = 1e-6

OFF_XA = 0
OFF_GA = LRU_WIDTH
OFF_XB = 2 * LRU_WIDTH
OFF_GB = OFF_XB + POOL_WIDTH
OFF_MA = OFF_GB + POOL_WIDTH
OFF_MB = OFF_MA + D_MODEL
IN_COLS = OFF_MB + D_MODEL

TT = 32
ROWS = TT * BATCH
CW = 256
N_CHUNKS = D_MODEL // CW
HEADS_PER_CHUNK = CW // LRU_HEAD_DIM
RT = 64
NORM_RT = 32
CONV_TAIL = (CONV_WIDTH - 1) * BATCH
POOL_TAIL = MAX_WIN * BATCH
X_SLOTS = 3
COMPILER_VMEM_ALLOWANCE_BYTES = 4 * 1024 * 1024

TB_GA = 0
TB_XC, TB_G0, TB_G1, TB_A, TB_U = 4, 6, 8, 10, 12
TB_MA, TB_MB = 14, 18
TB_GB = 22
N_TB = 24
TB_PE = 6

F32 = jnp.float32
BF16 = jnp.bfloat16
TINY = 1e-30


def _tanh_p1(vh):
    return jnp.tanh(vh) + 1.0


def _dot(a, b):
    return jnp.dot(a, b, preferred_element_type=F32)


def _block_kernel(x_hbm, p_hbm, ng_ref, win_hbm, cw_ref, cb_ref, wa_hbm, ba_ref, wx_hbm, bx_ref,
                  lam_ref, pw_hbm, ps_ref, plru_hbm, ppool_hbm, wout_hbm, pg_ref, wpg_hbm,
                  wpe_hbm, fg_ref,
                  o_hbm,
                  xbuf, pbuf, obuf, sem_x, sem_p, sem_o, sem_w,
                  win_ref, wax_ref, pw_ref, plru_ref, ppool_ref, wout_ref, wpg_ref, wpe_ref,
                  hb_in, hb2, pbf, xa_ext, xb_ext, hst, tb, xcb, ya, yb, dbuf, ybuf, mbuf, x1):
    ti = pl.program_id(0)
    n_steps = pl.num_programs(0)
    slot = ti % 2
    nxt = jnp.minimum(ti + 1, n_steps - 1)
    nxt2 = jnp.minimum(ti + 2, n_steps - 1)
    nslot = 1 - slot
    xs, xs1, xs2 = ti % X_SLOTS, (ti + 1) % X_SLOTS, (ti + 2) % X_SLOTS

    def x_copy(step, sl, b):
        return pltpu.make_async_copy(x_hbm.at[b, pl.ds(step * TT, TT), :], xbuf.at[sl, :, b, :],
                                     sem_x.at[sl])

    def p_copy(step, sl, b):
        return pltpu.make_async_copy(p_hbm.at[0, b, pl.ds(step * TT, TT), :], pbuf.at[sl, :, b, :],
                                     sem_p.at[sl])

    def o_copy(step, sl, b):
        return pltpu.make_async_copy(obuf.at[sl, :, b, :], o_hbm.at[b, pl.ds(step * TT, TT), :],
                                     sem_o.at[sl])

    def start_copies(copy, step, sl):
        for b in range(BATCH):
            copy(step, sl, b).start()

    def wait_slot(buf, sem, sl):
        pltpu.make_async_copy(buf.at[sl], buf.at[sl], sem.at[sl]).wait()

    def x_rows(sl, r0, nrows, c0=0, ncols=D_MODEL):
        t0, nt = r0 // BATCH, nrows // BATCH
        return xbuf[sl, pl.ds(t0, nt), :, c0:c0 + ncols].reshape(nrows, ncols)

    def rmsnorm(xv, g_ref, cast):
        ms = jnp.mean(xv * xv, axis=-1, keepdims=True)
        return ((xv * lax.rsqrt(ms + EPS)) * g_ref[...]).astype(cast)

    def win(off, c):
        return win_ref[:, off + c * CW:off + (c + 1) * CW]

    def norm_in(xsl, sl):
        for r0 in range(0, ROWS, NORM_RT):
            hb_in[sl, pl.ds(r0, NORM_RT), :] = rmsnorm(x_rows(xsl, r0, NORM_RT), ng_ref, BF16)

    def dots_xa_ga(sl, c):
        xa_ext[c, CONV_TAIL:CONV_TAIL + ROWS, :] = _dot(hb_in[sl], win(OFF_XA, c))
        tb[TB_GA + c] = _dot(hb_in[sl], win(OFF_GA, c))

    def weight_tiles():
        tiles = []

        def add(src, dst, nrows, ncols, scale):
            for r0 in range(0, nrows, ROWS):
                nr = min(ROWS, nrows - r0)
                for c0 in range(0, ncols, CW):
                    tiles.append((src.at[0, pl.ds(r0, nr), pl.ds(c0, CW)], nr, CW,
                                  dst.at[pl.ds(r0, nr), pl.ds(c0, CW)],
                                  scale(c0) if callable(scale) else scale))

        halved_in = lambda c0: 0.5 if (OFF_GA <= c0 < OFF_XB or c0 >= OFF_GB) else 1.0
        add(win_hbm, win_ref, D_MODEL, IN_COLS, halved_in)
        add(plru_hbm, plru_ref, LRU_WIDTH, D_MODEL, 0.5)
        add(ppool_hbm, ppool_ref, POOL_WIDTH, D_MODEL, 0.5)
        add(wout_hbm, wout_ref, D_MODEL, D_MODEL, 1.0)
        add(wpg_hbm, wpg_ref, D_MODEL, D_MODEL, 0.5)
        add(wpe_hbm, wpe_ref, P_DIM, D_MODEL, 0.5)
        for h in range(LRU_HEADS):
            for part, src in enumerate((wa_hbm, wx_hbm)):
                tiles.append((src.at[0, h], LRU_HEAD_DIM, LRU_HEAD_DIM,
                              wax_ref.at[h, :, pl.ds(part * LRU_HEAD_DIM, LRU_HEAD_DIM)], 0.5))
        for g in range(POOL_GROUPS):
            tiles.append((pw_hbm.at[0, g], POOL_GROUP_DIM, POOL_GROUP_DIM, pw_ref.at[g], 1.0))
        return tiles

    def load_weights():
        tiles = weight_tiles()
        copies = [pltpu.make_async_copy(src, tb.at[i % N_TB, pl.ds(0, nr), pl.ds(0, nc)], sem_w.at[i % N_TB])
                  for i, (src, nr, nc, _, _) in enumerate(tiles)]
        for i in range(len(tiles) + N_TB):
            done = i - N_TB
            if done >= 0:
                _, nr, nc, dst, scale = tiles[done]
                copies[done].wait()
                v = tb[done % N_TB, 0:nr, 0:nc]
                dst[...] = (v if scale == 1.0 else v * scale).astype(BF16)
            if i < len(tiles):
                copies[i].start()

    @pl.when(ti == 0)
    def _():
        start_copies(x_copy, 0, 0)
        start_copies(x_copy, 1, 1)
        start_copies(p_copy, 0, 0)
        start_copies(p_copy, 1, 1)
        load_weights()
        xa_ext[:, 0:CONV_TAIL, :] = jnp.zeros((N_CHUNKS, CONV_TAIL, CW), F32)
        xb_ext[0:POOL_TAIL, :] = jnp.zeros((POOL_TAIL, POOL_WIDTH), F32)
        hst[...] = jnp.zeros((BATCH, LRU_WIDTH), F32)
        wait_slot(xbuf, sem_x, 0)
        norm_in(0, 0)
        dots_xa_ga(0, 0)
        dots_xa_ga(0, 1)

    @pl.when(ti >= 1)
    def _():
        start_copies(o_copy, ti - 1, nslot)

    @pl.when(ti >= 2)
    def _():
        wait_slot(obuf, sem_o, slot)

    wait_slot(pbuf, sem_p, xs)
    wait_slot(xbuf, sem_x, xs1)

    hb = hb_in.at[slot]
    for r0 in range(0, ROWS, RT):
        pv = pbuf[xs, pl.ds(r0 // BATCH, RT // BATCH), :, :].reshape(RT, P_DIM)
        pbf[pl.ds(r0, RT), :] = pv.astype(BF16)

    lam = lam_ref[...]
    half_c = (-0.5 * LRU_C) * (jnp.maximum(-lam, 0.0) + jnp.log1p(jnp.exp(-jnp.abs(lam))))

    def conv(c):
        q, c0 = c % 2, c * CW
        for r0 in range(0, ROWS, RT):
            rows = pl.ds(r0, RT)
            acc = cb_ref[:, c0:c0 + CW]
            for k in range(CONV_WIDTH):
                acc = acc + xa_ext[c, r0 + k * BATCH:r0 + k * BATCH + RT, :] * cw_ref[k:k + 1, c0:c0 + CW]
            tb[TB_XC + q, rows, :] = acc
            xcb[q, rows, :] = acc.astype(BF16)
        xa_ext[c, 0:CONV_TAIL, :] = xa_ext[c, ROWS:ROWS + CONV_TAIL, :]

    def dots_gates(c):
        q = c % 2
        for hh, tg in enumerate((TB_G0, TB_G1)):
            l0 = hh * LRU_HEAD_DIM
            tb[tg + q] = _dot(xcb[q, :, l0:l0 + LRU_HEAD_DIM], wax_ref[c * HEADS_PER_CHUNK + hh])

    def gates(c):
        q = c % 2
        for hh, tg in enumerate((TB_G0, TB_G1)):
            l0 = hh * LRU_HEAD_DIM
            ch = slice(c * CW + l0, c * CW + l0 + LRU_HEAD_DIM)
            lanes = slice(l0, l0 + LRU_HEAD_DIM)
            for r0 in range(0, ROWS, RT):
                rows = pl.ds(r0, RT)
                r_tanh = jnp.tanh(tb[tg + q, rows, 0:LRU_HEAD_DIM] + 0.5 * ba_ref[:, ch])
                i_gate = 0.5 * jnp.tanh(tb[tg + q, rows, LRU_HEAD_DIM:2 * LRU_HEAD_DIM] + 0.5 * bx_ref[:, ch]) + 0.5
                a = jnp.exp(half_c[:, ch] * r_tanh + half_c[:, ch])
                y = 1.0 - a * a
                mult = y * lax.rsqrt(jnp.maximum(y, TINY))
                tb[TB_A + q, rows, lanes] = a
                tb[TB_U + q, rows, lanes] = mult * (i_gate * tb[TB_XC + q, rows, lanes])

    def scan(c):
        q, c0 = c % 2, c * CW
        h = hst[:, c0:c0 + CW]
        for t in range(TT):
            rows = pl.ds(t * BATCH, BATCH)
            h = tb[TB_A + q, rows, :] * h + tb[TB_U + q, rows, :]
            ga = tb[TB_GA + c, rows, :]
            ya[rows, c0:c0 + CW] = (h * (_tanh_p1(ga) * ga)).astype(BF16)
        hst[:, c0:c0 + CW] = h

    def dots_pool_in():
        xb_ext[POOL_TAIL:POOL_TAIL + ROWS, :] = _dot(hb[...], win_ref[:, OFF_XB:OFF_XB + POOL_WIDTH])
        for half in range(POOL_WIDTH // CW):
            tb[TB_GB + half] = _dot(hb[...], win(OFF_GB, half))

    def pool_windows(g):
        k = POOL_WINDOWS[g]
        lanes = slice(g * POOL_GROUP_DIM, (g + 1) * POOL_GROUP_DIM)
        for r0 in range(0, ROWS, RT):
            cur = xb_ext[POOL_TAIL + r0:POOL_TAIL + r0 + RT, lanes]
            s = cur
            for j in range(1, k):
                s = s + xb_ext[POOL_TAIL + r0 - j * BATCH:POOL_TAIL + r0 - j * BATCH + RT, lanes]
            row = lax.broadcasted_iota(jnp.int32, (RT, POOL_GROUP_DIM), 0) + r0
            pos = ti * TT + row // BATCH
            cnt = jnp.minimum(pos + 1, k).astype(F32)
            dbuf[g, r0:r0 + RT, :] = (s / cnt - cur).astype(BF16)

    def dot_pool(g):
        ybuf[g] = _dot(dbuf[g], pw_ref[g])

    def pool_out(g):
        lanes = slice(g * POOL_GROUP_DIM, (g + 1) * POOL_GROUP_DIM)
        half, gg = divmod(g, CW // POOL_GROUP_DIM)
        for r0 in range(0, ROWS, RT):
            rows = pl.ds(r0, RT)
            gb = tb[TB_GB + half, rows, gg * POOL_GROUP_DIM:(gg + 1) * POOL_GROUP_DIM]
            y = ybuf[g, rows, :] * ps_ref[:, lanes]
            yb[rows, lanes] = (y * (_tanh_p1(gb) * gb)).astype(BF16)

    def dots_merge_logits(c):
        tb[TB_MA + c] = _dot(hb[...], win(OFF_MA, c))
        tb[TB_MB + c] = _dot(hb[...], win(OFF_MB, c))

    for c in range(N_CHUNKS):
        conv(c)
        dots_gates(c)
        if c + 2 < N_CHUNKS:
            dots_xa_ga(slot, c + 2)
        if c == 0:
            dots_pool_in()
        else:
            dots_merge_logits(c - 1)
        gates(c)
        scan(c)
        pool_windows(c)
        dot_pool(c)
    dots_merge_logits(N_CHUNKS - 1)
    xb_ext[0:POOL_TAIL, :] = xb_ext[ROWS:ROWS + POOL_TAIL, :]
    for g in range(POOL_GROUPS):
        pool_out(g)

    for c in range(N_CHUNKS):
        c0 = c * CW
        pa = _dot(ya[...], plru_ref[:, c0:c0 + CW])
        pb = _dot(yb[...], ppool_ref[:, c0:c0 + CW])
        mbuf[:, c0:c0 + CW] = (_tanh_p1(tb[TB_MA + c]) * pa + _tanh_p1(tb[TB_MB + c]) * pb).astype(BF16)

    for c in range(N_CHUNKS):
        c0 = c * CW
        x1[:, c0:c0 + CW] = x_rows(xs, 0, ROWS, c0, CW) + _dot(mbuf[...], wout_ref[:, c0:c0 + CW])
    norm_in(xs1, nslot)
    start_copies(x_copy, nxt2, xs2)
    start_copies(p_copy, nxt2, xs2)

    for c in range(N_CHUNKS):
        tb[TB_PE + c] = _dot(pbf[...], wpe_ref[:, c * CW:(c + 1) * CW])
    for r0 in range(0, ROWS, NORM_RT):
        hb2[pl.ds(r0, NORM_RT), :] = rmsnorm(x1[pl.ds(r0, NORM_RT), :], pg_ref, BF16)

    for c in range(N_CHUNKS):
        c0 = c * CW
        gate = _dot(hb2[...], wpg_ref[:, c0:c0 + CW])
        x1[:, c0:c0 + CW] = x1[:, c0:c0 + CW] + _tanh_p1(gate) * tb[TB_PE + c]
    dots_xa_ga(nslot, 0)
    dots_xa_ga(nslot, 1)

    for r0 in range(0, ROWS, NORM_RT):
        ov = rmsnorm(x1[pl.ds(r0, NORM_RT), :], fg_ref, F32)
        obuf[slot, pl.ds(r0 // BATCH, NORM_RT // BATCH), :, :] = ov.reshape(NORM_RT // BATCH, BATCH, D_MODEL)

    @pl.when(ti == n_steps - 1)
    def _():
        start_copies(o_copy, ti, slot)
        wait_slot(xbuf, sem_x, xs2)
        wait_slot(pbuf, sem_p, xs1)
        wait_slot(pbuf, sem_p, xs2)
        wait_slot(obuf, sem_o, nslot)
        wait_slot(obuf, sem_o, slot)


def _const_spec(shape):
    zeros = (0,) * len(shape)
    return pl.BlockSpec(shape, lambda i: zeros, pipeline_mode=pl.Buffered(1))


@jax.jit
def kernel(x, p, norm_g, w_in, conv_w, conv_b, lru_w_a, lru_b_a, lru_w_x, lru_b_x, lru_lambda,
           pool_w, pool_scale, w_proj_lru, w_proj_pool, w_out, ple_norm_g, w_ple_gate,
           w_ple_proj, final_g):
    assert x.shape == (BATCH, SEQ, D_MODEL) and p.shape == (1, BATCH, SEQ, P_DIM)
    row = lambda v: v.reshape(1, -1)
    operands = [
        x, p, row(norm_g[0]), w_in, conv_w[0], row(conv_b[0]), lru_w_a, row(lru_b_a[0]), lru_w_x,
        row(lru_b_x[0]), row(lru_lambda[0]), pool_w, row(pool_scale[0]), w_proj_lru, w_proj_pool,
        w_out, row(ple_norm_g[0]), w_ple_gate, w_ple_proj, row(final_g),
    ]
    in_specs = [pl.BlockSpec(memory_space=pl.ANY) if v.ndim > 2 else _const_spec(v.shape)
                for v in operands]
    scratch = [
        pltpu.VMEM((X_SLOTS, TT, BATCH, D_MODEL), F32),
        pltpu.VMEM((X_SLOTS, TT, BATCH, P_DIM), F32),
        pltpu.VMEM((2, TT, BATCH, D_MODEL), F32),
        pltpu.SemaphoreType.DMA((X_SLOTS,)),
        pltpu.SemaphoreType.DMA((X_SLOTS,)),
        pltpu.SemaphoreType.DMA((2,)),
        pltpu.SemaphoreType.DMA((N_TB,)),
        pltpu.VMEM((D_MODEL, IN_COLS), BF16),
        pltpu.VMEM((LRU_HEADS, LRU_HEAD_DIM, 2 * LRU_HEAD_DIM), BF16),
        pltpu.VMEM((POOL_GROUPS, POOL_GROUP_DIM, POOL_GROUP_DIM), BF16),
        pltpu.VMEM((LRU_WIDTH, D_MODEL), BF16),
        pltpu.VMEM((POOL_WIDTH, D_MODEL), BF16),
        pltpu.VMEM((D_MODEL, D_MODEL), BF16),
        pltpu.VMEM((D_MODEL, D_MODEL), BF16),
        pltpu.VMEM((P_DIM, D_MODEL), BF16),
        pltpu.VMEM((2, ROWS, D_MODEL), BF16),
        pltpu.VMEM((ROWS, D_MODEL), BF16),
        pltpu.VMEM((ROWS, P_DIM), BF16),
        pltpu.VMEM((N_CHUNKS, ROWS + CONV_TAIL, CW), F32),
        pltpu.VMEM((ROWS + POOL_TAIL, POOL_WIDTH), F32),
        pltpu.VMEM((BATCH, LRU_WIDTH), F32),
        pltpu.VMEM((N_TB, ROWS, CW), F32),
        pltpu.VMEM((2, ROWS, CW), BF16),
        pltpu.VMEM((ROWS, LRU_WIDTH), BF16),
        pltpu.VMEM((ROWS, POOL_WIDTH), BF16),
        pltpu.VMEM((POOL_GROUPS, ROWS, POOL_GROUP_DIM), BF16),
        pltpu.VMEM((POOL_GROUPS, ROWS, POOL_GROUP_DIM), F32),
        pltpu.VMEM((ROWS, D_MODEL), BF16),
        pltpu.VMEM((ROWS, D_MODEL), F32),
    ]
    scratch_vmem_bytes = sum(math.prod(s.shape) * jnp.dtype(s.dtype).itemsize
                             for s in scratch if s.memory_space == pltpu.VMEM)
    return pl.pallas_call(
        _block_kernel,
        grid=(SEQ // TT,),
        in_specs=in_specs,
        out_specs=pl.BlockSpec(memory_space=pl.ANY),
        out_shape=jax.ShapeDtypeStruct((BATCH, SEQ, D_MODEL), F32),
        scratch_shapes=scratch,
        compiler_params=pltpu.CompilerParams(
            dimension_semantics=("arbitrary",),
            vmem_limit_bytes=scratch_vmem_bytes + COMPILER_VMEM_ALLOWANCE_BYTES,
        ),
        name="rglru_pool_block",
    )(*operands)
```

```python
import math

import jax
import jax.numpy as jnp
from jax import lax
from jax.experimental import pallas as pl
from jax.experimental.pallas import tpu as pltpu

D_MODEL = 1024
BATCH = 16
SEQ = 2048
P_DIM = 256
LRU_WIDTH = 1024
LRU_HEADS = 8
LRU_HEAD_DIM = 128
CONV_WIDTH = 4
LRU_C = 8.0
POOL_WIDTH = 512
POOL_WINDOWS = (2, 4, 8, 16)
POOL_GROUPS = 4
POOL_GROUP_DIM = 128
MAX_WIN = 16
EPS = 1e-6

OFF_XA = 0
OFF_GA = LRU_WIDTH
OFF_XB = 2 * LRU_WIDTH
OFF_GB = OFF_XB + POOL_WIDTH
OFF_MA = OFF_GB + POOL_WIDTH
OFF_MB = OFF_MA + D_MODEL
IN_COLS = OFF_MB + D_MODEL

TT = 32
ROWS = TT * BATCH
CW = 256
N_CHUNKS = D_MODEL // CW
HEADS_PER_CHUNK = CW // LRU_HEAD_DIM
RT = 64
NORM_RT = 32
CONV_TAIL = (CONV_WIDTH - 1) * BATCH
POOL_TAIL = MAX_WIN * BATCH
X_SLOTS = 3
COMPILER_VMEM_ALLOWANCE_BYTES = 4 * 1024 * 1024

TB_GA = 0
TB_XC, TB_G0, TB_G1, TB_A, TB_U = 4, 6, 8, 10, 12
TB_MA, TB_MB = 14, 18
TB_GB = 22
N_TB = 24
TB_PE = 6

F32 = jnp.float32
BF16 = jnp.bfloat16
TINY = 1e-30


def _tanh_p1(vh):
    return jnp.tanh(vh) + 1.0


def _dot(a, b):
    return jnp.dot(a, b, preferred_element_type=F32)


def _block_kernel(x_hbm, p_hbm, ng_ref, win_hbm, cw_ref, cb_ref, wa_hbm, ba_ref, wx_hbm, bx_ref,
                  lam_ref, pw_hbm, ps_ref, plru_hbm, ppool_hbm, wout_hbm, pg_ref, wpg_hbm,
                  wpe_hbm, fg_ref,
                  o_hbm,
                  xbuf, pbuf, obuf, sem_x, sem_p, sem_o, sem_w,
                  win_ref, wax_ref, pw_ref, plru_ref, ppool_ref, wout_ref, wpg_ref, wpe_ref,
                  hb_in, hb2, pbf, xa_ext, xb_ext, hst, tb, xcb, ya, yb, dbuf, mbuf, x1):
    ti = pl.program_id(0)
    n_steps = pl.num_programs(0)
    slot = ti % 2
    nxt = jnp.minimum(ti + 1, n_steps - 1)
    nxt2 = jnp.minimum(ti + 2, n_steps - 1)
    nslot = 1 - slot
    xs, xs1, xs2 = ti % X_SLOTS, (ti + 1) % X_SLOTS, (ti + 2) % X_SLOTS

    def x_copy(step, sl, b):
        return pltpu.make_async_copy(x_hbm.at[b, pl.ds(step * TT, TT), :], xbuf.at[sl, :, b, :],
                                     sem_x.at[sl])

    def p_copy(step, sl, b):
        return pltpu.make_async_copy(p_hbm.at[0, b, pl.ds(step * TT, TT), :], pbuf.at[sl, :, b, :],
                                     sem_p.at[sl])

    def o_copy(step, sl, b):
        return pltpu.make_async_copy(obuf.at[sl, :, b, :], o_hbm.at[b, pl.ds(step * TT, TT), :],
                                     sem_o.at[sl])

    def start_copies(copy, step, sl):
        for b in range(BATCH):
            copy(step, sl, b).start()

    def wait_slot(buf, sem, sl):
        pltpu.make_async_copy(buf.at[sl], buf.at[sl], sem.at[sl]).wait()

    def x_rows(sl, r0, nrows, c0=0, ncols=D_MODEL):
        t0, nt = r0 // BATCH, nrows // BATCH
        return xbuf[sl, pl.ds(t0, nt), :, c0:c0 + ncols].reshape(nrows, ncols)

    def rmsnorm(xv, g_ref, cast):
        ms = jnp.mean(xv * xv, axis=-1, keepdims=True)
        return ((xv * lax.rsqrt(ms + EPS)) * g_ref[...]).astype(cast)

    def win(off, c):
        return win_ref[:, off + c * CW:off + (c + 1) * CW]

    def norm_in(xsl, sl):
        for r0 in range(0, ROWS, NORM_RT):
            hb_in[sl, pl.ds(r0, NORM_RT), :] = rmsnorm(x_rows(xsl, r0, NORM_RT), ng_ref, BF16)

    def dots_xa_ga(sl, c):
        xa_ext[c, CONV_TAIL:CONV_TAIL + ROWS, :] = _dot(hb_in[sl], win(OFF_XA, c))
        tb[TB_GA + c] = _dot(hb_in[sl], win(OFF_GA, c))

    def weight_tiles():
        tiles = []

        def add(src, dst, nrows, ncols, scale):
            for r0 in range(0, nrows, ROWS):
                nr = min(ROWS, nrows - r0)
                for c0 in range(0, ncols, CW):
                    tiles.append((src.at[0, pl.ds(r0, nr), pl.ds(c0, CW)], nr, CW,
                                  dst.at[pl.ds(r0, nr), pl.ds(c0, CW)],
                                  scale(c0) if callable(scale) else scale))

        halved_in = lambda c0: 0.5 if (OFF_GA <= c0 < OFF_XB or c0 >= OFF_GB) else 1.0
        add(win_hbm, win_ref, D_MODEL, IN_COLS, halved_in)
        add(plru_hbm, plru_ref, LRU_WIDTH, D_MODEL, 0.5)
        add(ppool_hbm, ppool_ref, POOL_WIDTH, D_MODEL, 0.5)
        add(wout_hbm, wout_ref, D_MODEL, D_MODEL, 1.0)
        add(wpg_hbm, wpg_ref, D_MODEL, D_MODEL, 0.5)
        add(wpe_hbm, wpe_ref, P_DIM, D_MODEL, 0.5)
        for h in range(LRU_HEADS):
            for part, src in enumerate((wa_hbm, wx_hbm)):
                tiles.append((src.at[0, h], LRU_HEAD_DIM, LRU_HEAD_DIM,
                              wax_ref.at[h, :, pl.ds(part * LRU_HEAD_DIM, LRU_HEAD_DIM)], 0.5))
        for g in range(POOL_GROUPS):
            tiles.append((pw_hbm.at[0, g], POOL_GROUP_DIM, POOL_GROUP_DIM, pw_ref.at[g], 1.0))
        return tiles

    def load_weights():
        tiles = weight_tiles()
        copies = [pltpu.make_async_copy(src, tb.at[i % N_TB, pl.ds(0, nr), pl.ds(0, nc)], sem_w.at[i % N_TB])
                  for i, (src, nr, nc, _, _) in enumerate(tiles)]
        for i in range(len(tiles) + N_TB):
            done = i - N_TB
            if done >= 0:
                _, nr, nc, dst, scale = tiles[done]
                copies[done].wait()
                v = tb[done % N_TB, 0:nr, 0:nc]
                dst[...] = (v if scale == 1.0 else v * scale).astype(BF16)
            if i < len(tiles):
                copies[i].start()

    @pl.when(ti == 0)
    def _():
        start_copies(x_copy, 0, 0)
        start_copies(x_copy, 1, 1)
        start_copies(p_copy, 0, 0)
        start_copies(p_copy, 1, 1)
        load_weights()
        xa_ext[:, 0:CONV_TAIL, :] = jnp.zeros((N_CHUNKS, CONV_TAIL, CW), F32)
        xb_ext[0:POOL_TAIL, :] = jnp.zeros((POOL_TAIL, POOL_WIDTH), F32)
        hst[...] = jnp.zeros((BATCH, LRU_WIDTH), F32)
        wait_slot(xbuf, sem_x, 0)
        norm_in(0, 0)
        dots_xa_ga(0, 0)
        dots_xa_ga(0, 1)

    @pl.when(ti >= 1)
    def _():
        start_copies(o_copy, ti - 1, nslot)

    @pl.when(ti >= 2)
    def _():
        wait_slot(obuf, sem_o, slot)

    wait_slot(pbuf, sem_p, xs)
    wait_slot(xbuf, sem_x, xs1)

    hb = hb_in.at[slot]
    for r0 in range(0, ROWS, RT):
        pv = pbuf[xs, pl.ds(r0 // BATCH, RT // BATCH), :, :].reshape(RT, P_DIM)
        pbf[pl.ds(r0, RT), :] = pv.astype(BF16)

    lam = lam_ref[...]
    half_c = (-0.5 * LRU_C) * (jnp.maximum(-lam, 0.0) + jnp.log1p(jnp.exp(-jnp.abs(lam))))

    def conv(c):
        q, c0 = c % 2, c * CW
        acc = cb_ref[:, c0:c0 + CW]
        for k in range(CONV_WIDTH):
            acc = acc + xa_ext[c, k * BATCH:k * BATCH + ROWS, :] * cw_ref[k:k + 1, c0:c0 + CW]
        tb[TB_XC + q] = acc
        xcb[q] = acc.astype(BF16)
        xa_ext[c, 0:CONV_TAIL, :] = xa_ext[c, ROWS:ROWS + CONV_TAIL, :]

    def dots_gates(c):
        q = c % 2
        for hh, tg in enumerate((TB_G0, TB_G1)):
            l0 = hh * LRU_HEAD_DIM
            tb[tg + q] = _dot(xcb[q, :, l0:l0 + LRU_HEAD_DIM], wax_ref[c * HEADS_PER_CHUNK + hh])

    def gates(c):
        q = c % 2
        for hh, tg in enumerate((TB_G0, TB_G1)):
            l0 = hh * LRU_HEAD_DIM
            ch = slice(c * CW + l0, c * CW + l0 + LRU_HEAD_DIM)
            lanes = slice(l0, l0 + LRU_HEAD_DIM)
            for r0 in range(0, ROWS, RT):
                rows = pl.ds(r0, RT)
                r_tanh = jnp.tanh(tb[tg + q, rows, 0:LRU_HEAD_DIM] + 0.5 * ba_ref[:, ch])
                i_gate = 0.5 * jnp.tanh(tb[tg + q, rows, LRU_HEAD_DIM:2 * LRU_HEAD_DIM] + 0.5 * bx_ref[:, ch]) + 0.5
                a = jnp.exp(half_c[:, ch] * r_tanh + half_c[:, ch])
                y = 1.0 - a * a
                mult = y * lax.rsqrt(jnp.maximum(y, TINY))
                tb[TB_A + q, rows, lanes] = a
                tb[TB_U + q, rows, lanes] = mult * (i_gate * tb[TB_XC + q, rows, lanes])

    def scan(c):
        q, c0 = c % 2, c * CW
        h = hst[:, c0:c0 + CW]
        for t in range(TT):
            rows = pl.ds(t * BATCH, BATCH)
            h = tb[TB_A + q, rows, :] * h + tb[TB_U + q, rows, :]
            ga = tb[TB_GA + c, rows, :]
            ya[rows, c0:c0 + CW] = (h * (_tanh_p1(ga) * ga)).astype(BF16)
        hst[:, c0:c0 + CW] = h

    def dots_pool_in():
        xb_ext[POOL_TAIL:POOL_TAIL + ROWS, :] = _dot(hb[...], win_ref[:, OFF_XB:OFF_XB + POOL_WIDTH])
        for half in range(POOL_WIDTH // CW):
            tb[TB_GB + half] = _dot(hb[...], win(OFF_GB, half))

    def pool_windows(g):
        k = POOL_WINDOWS[g]
        lanes = slice(g * POOL_GROUP_DIM, (g + 1) * POOL_GROUP_DIM)
        for r0 in range(0, ROWS, RT):
            cur = xb_ext[POOL_TAIL + r0:POOL_TAIL + r0 + RT, lanes]
            s = cur
            for j in range(1, k):
                s = s + xb_ext[POOL_TAIL + r0 - j * BATCH:POOL_TAIL + r0 - j * BATCH + RT, lanes]
            row = lax.broadcasted_iota(jnp.int32, (RT, POOL_GROUP_DIM), 0) + r0
            pos = ti * TT + row // BATCH
            cnt = jnp.minimum(pos + 1, k).astype(F32)
            dbuf[g, r0:r0 + RT, :] = (s / cnt - cur).astype(BF16)

    def pool_out(g):
        lanes = slice(g * POOL_GROUP_DIM, (g + 1) * POOL_GROUP_DIM)
        half, gg = divmod(g, CW // POOL_GROUP_DIM)
        gb = tb[TB_GB + half, :, gg * POOL_GROUP_DIM:(gg + 1) * POOL_GROUP_DIM]
        y = _dot(dbuf[g], pw_ref[g]) * ps_ref[:, lanes]
        yb[:, lanes] = (y * (_tanh_p1(gb) * gb)).astype(BF16)

    def dots_merge_logits(c):
        tb[TB_MA + c] = _dot(hb[...], win(OFF_MA, c))
        tb[TB_MB + c] = _dot(hb[...], win(OFF_MB, c))

    for c in range(N_CHUNKS):
        conv(c)
        dots_gates(c)
        if c + 2 < N_CHUNKS:
            dots_xa_ga(slot, c + 2)
        if c == 0:
            dots_pool_in()
        else:
            dots_merge_logits(c - 1)
        gates(c)
        scan(c)
        pool_windows(c)
        pool_out(c)
    dots_merge_logits(N_CHUNKS - 1)
    xb_ext[0:POOL_TAIL, :] = xb_ext[ROWS:ROWS + POOL_TAIL, :]

    for c in range(N_CHUNKS):
        c0 = c * CW
        pa = _dot(ya[...], plru_ref[:, c0:c0 + CW])
        pb = _dot(yb[...], ppool_ref[:, c0:c0 + CW])
        mbuf[:, c0:c0 + CW] = (_tanh_p1(tb[TB_MA + c]) * pa + _tanh_p1(tb[TB_MB + c]) * pb).astype(BF16)

    for c in range(N_CHUNKS):
        c0 = c * CW
        x1[:, c0:c0 + CW] = x_rows(xs, 0, ROWS, c0, CW) + _dot(mbuf[...], wout_ref[:, c0:c0 + CW])
    norm_in(xs1, nslot)
    start_copies(x_copy, nxt2, xs2)
    start_copies(p_copy, nxt2, xs2)

    for c in range(N_CHUNKS):
        tb[TB_PE + c] = _dot(pbf[...], wpe_ref[:, c * CW:(c + 1) * CW])
    for r0 in range(0, ROWS, NORM_RT):
        hb2[pl.ds(r0, NORM_RT), :] = rmsnorm(x1[pl.ds(r0, NORM_RT), :], pg_ref, BF16)

    for c in range(N_CHUNKS):
        c0 = c * CW
        gate = _dot(hb2[...], wpg_ref[:, c0:c0 + CW])
        x1[:, c0:c0 + CW] = x1[:, c0:c0 + CW] + _tanh_p1(gate) * tb[TB_PE + c]
    dots_xa_ga(nslot, 0)
    dots_xa_ga(nslot, 1)

    for r0 in range(0, ROWS, NORM_RT):
        ov = rmsnorm(x1[pl.ds(r0, NORM_RT), :], fg_ref, F32)
        obuf[slot, pl.ds(r0 // BATCH, NORM_RT // BATCH), :, :] = ov.reshape(NORM_RT // BATCH, BATCH, D_MODEL)

    @pl.when(ti == n_steps - 1)
    def _():
        start_copies(o_copy, ti, slot)
        wait_slot(xbuf, sem_x, xs2)
        wait_slot(pbuf, sem_p, xs1)
        wait_slot(pbuf, sem_p, xs2)
        wait_slot(obuf, sem_o, nslot)
        wait_slot(obuf, sem_o, slot)


def _const_spec(shape):
    zeros = (0,) * len(shape)
    return pl.BlockSpec(shape, lambda i: zeros, pipeline_mode=pl.Buffered(1))


@jax.jit
def kernel(x, p, norm_g, w_in, conv_w, conv_b, lru_w_a, lru_b_a, lru_w_x, lru_b_x, lru_lambda,
           pool_w, pool_scale, w_proj_lru, w_proj_pool, w_out, ple_norm_g, w_ple_gate,
           w_ple_proj, final_g):
    assert x.shape == (BATCH, SEQ, D_MODEL) and p.shape == (1, BATCH, SEQ, P_DIM)
    row = lambda v: v.reshape(1, -1)
    operands = [
        x, p, row(norm_g[0]), w_in, conv_w[0], row(conv_b[0]), lru_w_a, row(lru_b_a[0]), lru_w_x,
        row(lru_b_x[0]), row(lru_lambda[0]), pool_w, row(pool_scale[0]), w_proj_lru, w_proj_pool,
        w_out, row(ple_norm_g[0]), w_ple_gate, w_ple_proj, row(final_g),
    ]
    in_specs = [pl.BlockSpec(memory_space=pl.ANY) if v.ndim > 2 else _const_spec(v.shape)
                for v in operands]
    scratch = [
        pltpu.VMEM((X_SLOTS, TT, BATCH, D_MODEL), F32),
        pltpu.VMEM((X_SLOTS, TT, BATCH, P_DIM), F32),
        pltpu.VMEM((2, TT, BATCH, D_MODEL), F32),
        pltpu.SemaphoreType.DMA((X_SLOTS,)),
        pltpu.SemaphoreType.DMA((X_SLOTS,)),
        pltpu.SemaphoreType.DMA((2,)),
        pltpu.SemaphoreType.DMA((N_TB,)),
        pltpu.VMEM((D_MODEL, IN_COLS), BF16),
        pltpu.VMEM((LRU_HEADS, LRU_HEAD_DIM, 2 * LRU_HEAD_DIM), BF16),
        pltpu.VMEM((POOL_GROUPS, POOL_GROUP_DIM, POOL_GROUP_DIM), BF16),
        pltpu.VMEM((LRU_WIDTH, D_MODEL), BF16),
        pltpu.VMEM((POOL_WIDTH, D_MODEL), BF16),
        pltpu.VMEM((D_MODEL, D_MODEL), BF16),
        pltpu.VMEM((D_MODEL, D_MODEL), BF16),
        pltpu.VMEM((P_DIM, D_MODEL), BF16),
        pltpu.VMEM((2, ROWS, D_MODEL), BF16),
        pltpu.VMEM((ROWS, D_MODEL), BF16),
        pltpu.VMEM((ROWS, P_DIM), BF16),
        pltpu.VMEM((N_CHUNKS, ROWS + CONV_TAIL, CW), F32),
        pltpu.VMEM((ROWS + POOL_TAIL, POOL_WIDTH), F32),
        pltpu.VMEM((BATCH, LRU_WIDTH), F32),
        pltpu.VMEM((N_TB, ROWS, CW), F32),
        pltpu.VMEM((2, ROWS, CW), BF16),
        pltpu.VMEM((ROWS, LRU_WIDTH), BF16),
        pltpu.VMEM((ROWS, POOL_WIDTH), BF16),
        pltpu.VMEM((POOL_GROUPS, ROWS, POOL_GROUP_DIM), BF16),
        pltpu.VMEM((ROWS, D_MODEL), BF16),
        pltpu.VMEM((ROWS, D_MODEL), F32),
    ]
    scratch_vmem_bytes = sum(math.prod(s.shape) * jnp.dtype(s.dtype).itemsize
                             for s in scratch if s.memory_space == pltpu.VMEM)
    return pl.pallas_call(
        _block_kernel,
        grid=(SEQ // TT,),
        in_specs=in_specs,
        out_specs=pl.BlockSpec(memory_space=pl.ANY),
        out_shape=jax.ShapeDtypeStruct((BATCH, SEQ, D_MODEL), F32),
        scratch_shapes=scratch,
        compiler_params=pltpu.CompilerParams(
            dimension_semantics=("arbitrary",),
            vmem_limit_bytes=scratch_vmem_bytes + COMPILER_VMEM_ALLOWANCE_BYTES,
        ),
        name="rglru_pool_block",
    )(*operands)
```

```python
import math

import jax
import jax.numpy as jnp
from jax import lax
from jax.experimental import pallas as pl
from jax.experimental.pallas import tpu as pltpu

D_MODEL = 1024
BATCH = 16
SEQ = 2048
P_DIM = 256
LRU_WIDTH = 1024
LRU_HEADS = 8
LRU_HEAD_DIM = 128
CONV_WIDTH = 4
LRU_C = 8.0
POOL_WIDTH = 512
POOL_WINDOWS = (2, 4, 8, 16)
POOL_GROUPS = 4
POOL_GROUP_DIM = 128
MAX_WIN = 16
EPS = 1e-6

OFF_XA = 0
OFF_GA = LRU_WIDTH
OFF_XB = 2 * LRU_WIDTH
OFF_GB = OFF_XB + POOL_WIDTH
OFF_MA = OFF_GB + POOL_WIDTH
OFF_MB = OFF_MA + D_MODEL
IN_COLS = OFF_MB + D_MODEL

TT = 32
ROWS = TT * BATCH
CW = 256
N_CHUNKS = D_MODEL // CW
HEADS_PER_CHUNK = CW // LRU_HEAD_DIM
NORM_RT = 32
CONV_TAIL = (CONV_WIDTH - 1) * BATCH
POOL_TAIL = MAX_WIN * BATCH
X_SLOTS = 3
COMPILER_VMEM_ALLOWANCE_BYTES = 4 * 1024 * 1024

TB_GA = 0
TB_XC, TB_G0, TB_G1, TB_A, TB_U = 4, 6, 8, 10, 12
TB_MA, TB_MB = 14, 18
TB_GB = 22
N_TB = 24
TB_PE = 6

F32 = jnp.float32
BF16 = jnp.bfloat16
TINY = 1e-30


def _tanh_p1(vh):
    return jnp.tanh(vh) + 1.0


def _dot(a, b):
    return jnp.dot(a, b, preferred_element_type=F32)


def _block_kernel(x_hbm, p_hbm, ng_ref, win_hbm, cw_ref, cb_ref, wa_hbm, ba_ref, wx_hbm, bx_ref,
                  lam_ref, pw_hbm, ps_ref, plru_hbm, ppool_hbm, wout_hbm, pg_ref, wpg_hbm,
                  wpe_hbm, fg_ref,
                  o_hbm,
                  xbuf, pbuf, obuf, sem_x, sem_p, sem_o, sem_w,
                  win_ref, wax_ref, pw_ref, plru_ref, ppool_ref, wout_ref, wpg_ref, wpe_ref,
                  hb_in, hb2, pbf, xa_ext, xb_ext, hst, tb, xcb, ya, yb, dbuf, mbuf, x1):
    ti = pl.program_id(0)
    n_steps = pl.num_programs(0)
    slot = ti % 2
    nxt = jnp.minimum(ti + 1, n_steps - 1)
    nxt2 = jnp.minimum(ti + 2, n_steps - 1)
    nslot = 1 - slot
    xs, xs1, xs2 = ti % X_SLOTS, (ti + 1) % X_SLOTS, (ti + 2) % X_SLOTS

    def x_copy(step, sl, b):
        return pltpu.make_async_copy(x_hbm.at[b, pl.ds(step * TT, TT), :], xbuf.at[sl, :, b, :],
                                     sem_x.at[sl])

    def p_copy(step, sl, b):
        return pltpu.make_async_copy(p_hbm.at[0, b, pl.ds(step * TT, TT), :], pbuf.at[sl, :, b, :],
                                     sem_p.at[sl])

    def o_copy(step, sl, b):
        return pltpu.make_async_copy(obuf.at[sl, :, b, :], o_hbm.at[b, pl.ds(step * TT, TT), :],
                                     sem_o.at[sl])

    def start_copies(copy, step, sl):
        for b in range(BATCH):
            copy(step, sl, b).start()

    def wait_slot(buf, sem, sl):
        pltpu.make_async_copy(buf.at[sl], buf.at[sl], sem.at[sl]).wait()

    def x_rows(sl, r0, nrows, c0=0, ncols=D_MODEL):
        t0, nt = r0 // BATCH, nrows // BATCH
        return xbuf[sl, pl.ds(t0, nt), :, c0:c0 + ncols].reshape(nrows, ncols)

    def rmsnorm(xv, g_ref, cast):
        ms = jnp.mean(xv * xv, axis=-1, keepdims=True)
        return ((xv * lax.rsqrt(ms + EPS)) * g_ref[...]).astype(cast)

    def win(off, c):
        return win_ref[:, off + c * CW:off + (c + 1) * CW]

    def norm_in(xsl, sl):
        for r0 in range(0, ROWS, NORM_RT):
            hb_in[sl, pl.ds(r0, NORM_RT), :] = rmsnorm(x_rows(xsl, r0, NORM_RT), ng_ref, BF16)

    def dots_xa_ga(sl, c):
        xa_ext[c, CONV_TAIL:CONV_TAIL + ROWS, :] = _dot(hb_in[sl], win(OFF_XA, c))
        tb[TB_GA + c] = _dot(hb_in[sl], win(OFF_GA, c))

    def weight_tiles():
        tiles = []

        def add(src, dst, nrows, ncols, scale):
            for r0 in range(0, nrows, ROWS):
                nr = min(ROWS, nrows - r0)
                for c0 in range(0, ncols, CW):
                    tiles.append((src.at[0, pl.ds(r0, nr), pl.ds(c0, CW)], nr, CW,
                                  dst.at[pl.ds(r0, nr), pl.ds(c0, CW)],
                                  scale(c0) if callable(scale) else scale))

        halved_in = lambda c0: 0.5 if (OFF_GA <= c0 < OFF_XB or c0 >= OFF_GB) else 1.0
        add(win_hbm, win_ref, D_MODEL, IN_COLS, halved_in)
        add(plru_hbm, plru_ref, LRU_WIDTH, D_MODEL, 0.5)
        add(ppool_hbm, ppool_ref, POOL_WIDTH, D_MODEL, 0.5)
        add(wout_hbm, wout_ref, D_MODEL, D_MODEL, 1.0)
        add(wpg_hbm, wpg_ref, D_MODEL, D_MODEL, 0.5)
        add(wpe_hbm, wpe_ref, P_DIM, D_MODEL, 0.5)
        for h in range(LRU_HEADS):
            for part, src in enumerate((wa_hbm, wx_hbm)):
                tiles.append((src.at[0, h], LRU_HEAD_DIM, LRU_HEAD_DIM,
                              wax_ref.at[h, :, pl.ds(part * LRU_HEAD_DIM, LRU_HEAD_DIM)], 0.5))
        for g in range(POOL_GROUPS):
            tiles.append((pw_hbm.at[0, g], POOL_GROUP_DIM, POOL_GROUP_DIM, pw_ref.at[g], 1.0))
        return tiles

    def load_weights():
        tiles = weight_tiles()
        copies = [pltpu.make_async_copy(src, tb.at[i % N_TB, pl.ds(0, nr), pl.ds(0, nc)], sem_w.at[i % N_TB])
                  for i, (src, nr, nc, _, _) in enumerate(tiles)]
        for i in range(len(tiles) + N_TB):
            done = i - N_TB
            if done >= 0:
                _, nr, nc, dst, scale = tiles[done]
                copies[done].wait()
                v = tb[done % N_TB, 0:nr, 0:nc]
                dst[...] = (v if scale == 1.0 else v * scale).astype(BF16)
            if i < len(tiles):
                copies[i].start()

    @pl.when(ti == 0)
    def _():
        start_copies(x_copy, 0, 0)
        start_copies(x_copy, 1, 1)
        start_copies(p_copy, 0, 0)
        start_copies(p_copy, 1, 1)
        load_weights()
        xa_ext[:, 0:CONV_TAIL, :] = jnp.zeros((N_CHUNKS, CONV_TAIL, CW), F32)
        xb_ext[0:POOL_TAIL, :] = jnp.zeros((POOL_TAIL, POOL_WIDTH), F32)
        hst[...] = jnp.zeros((BATCH, LRU_WIDTH), F32)
        wait_slot(xbuf, sem_x, 0)
        norm_in(0, 0)
        dots_xa_ga(0, 0)
        dots_xa_ga(0, 1)

    @pl.when(ti >= 1)
    def _():
        start_copies(o_copy, ti - 1, nslot)

    @pl.when(ti >= 2)
    def _():
        wait_slot(obuf, sem_o, slot)

    wait_slot(pbuf, sem_p, xs)
    wait_slot(xbuf, sem_x, xs1)

    hb = hb_in.at[slot]
    pbf[...] = pbuf[xs].reshape(ROWS, P_DIM).astype(BF16)

    lam = lam_ref[...]
    half_c = (-0.5 * LRU_C) * (jnp.maximum(-lam, 0.0) + jnp.log1p(jnp.exp(-jnp.abs(lam))))

    def conv(c):
        q, c0 = c % 2, c * CW
        acc = cb_ref[:, c0:c0 + CW]
        for k in range(CONV_WIDTH):
            acc = acc + xa_ext[c, k * BATCH:k * BATCH + ROWS, :] * cw_ref[k:k + 1, c0:c0 + CW]
        tb[TB_XC + q] = acc
        xcb[q] = acc.astype(BF16)
        xa_ext[c, 0:CONV_TAIL, :] = xa_ext[c, ROWS:ROWS + CONV_TAIL, :]

    def dots_gates(c):
        q = c % 2
        for hh, tg in enumerate((TB_G0, TB_G1)):
            l0 = hh * LRU_HEAD_DIM
            tb[tg + q] = _dot(xcb[q, :, l0:l0 + LRU_HEAD_DIM], wax_ref[c * HEADS_PER_CHUNK + hh])

    def gates(c):
        q = c % 2
        for hh, tg in enumerate((TB_G0, TB_G1)):
            l0 = hh * LRU_HEAD_DIM
            ch = slice(c * CW + l0, c * CW + l0 + LRU_HEAD_DIM)
            lanes = slice(l0, l0 + LRU_HEAD_DIM)
            r_tanh = jnp.tanh(tb[tg + q, :, 0:LRU_HEAD_DIM] + 0.5 * ba_ref[:, ch])
            i_gate = 0.5 * jnp.tanh(tb[tg + q, :, LRU_HEAD_DIM:2 * LRU_HEAD_DIM] + 0.5 * bx_ref[:, ch]) + 0.5
            a = jnp.exp(half_c[:, ch] * r_tanh + half_c[:, ch])
            y = 1.0 - a * a
            mult = y * lax.rsqrt(jnp.maximum(y, TINY))
            tb[TB_A + q, :, lanes] = a
            tb[TB_U + q, :, lanes] = mult * (i_gate * tb[TB_XC + q, :, lanes])

    def scan(c):
        q, c0 = c % 2, c * CW
        h = hst[:, c0:c0 + CW]
        for t in range(TT):
            rows = pl.ds(t * BATCH, BATCH)
            h = tb[TB_A + q, rows, :] * h + tb[TB_U + q, rows, :]
            ga = tb[TB_GA + c, rows, :]
            ya[rows, c0:c0 + CW] = (h * (_tanh_p1(ga) * ga)).astype(BF16)
        hst[:, c0:c0 + CW] = h

    def dots_pool_in():
        xb_ext[POOL_TAIL:POOL_TAIL + ROWS, :] = _dot(hb[...], win_ref[:, OFF_XB:OFF_XB + POOL_WIDTH])
        for half in range(POOL_WIDTH // CW):
            tb[TB_GB + half] = _dot(hb[...], win(OFF_GB, half))

    def pool_windows(g):
        k = POOL_WINDOWS[g]
        lanes = slice(g * POOL_GROUP_DIM, (g + 1) * POOL_GROUP_DIM)
        cur = xb_ext[POOL_TAIL:POOL_TAIL + ROWS, lanes]
        s = cur
        for j in range(1, k):
            s = s + xb_ext[POOL_TAIL - j * BATCH:POOL_TAIL - j * BATCH + ROWS, lanes]
        row = lax.broadcasted_iota(jnp.int32, (ROWS, POOL_GROUP_DIM), 0)
        pos = ti * TT + row // BATCH
        cnt = jnp.minimum(pos + 1, k).astype(F32)
        dbuf[g] = (s / cnt - cur).astype(BF16)

    def pool_out(g):
        lanes = slice(g * POOL_GROUP_DIM, (g + 1) * POOL_GROUP_DIM)
        half, gg = divmod(g, CW // POOL_GROUP_DIM)
        gb = tb[TB_GB + half, :, gg * POOL_GROUP_DIM:(gg + 1) * POOL_GROUP_DIM]
        y = _dot(dbuf[g], pw_ref[g]) * ps_ref[:, lanes]
        yb[:, lanes] = (y * (_tanh_p1(gb) * gb)).astype(BF16)

    def dots_merge_logits(c):
        tb[TB_MA + c] = _dot(hb[...], win(OFF_MA, c))
        tb[TB_MB + c] = _dot(hb[...], win(OFF_MB, c))

    for c in range(N_CHUNKS):
        conv(c)
        dots_gates(c)
        if c + 2 < N_CHUNKS:
            dots_xa_ga(slot, c + 2)
        if c == 0:
            dots_pool_in()
        else:
            dots_merge_logits(c - 1)
        gates(c)
        scan(c)
        pool_windows(c)
        pool_out(c)
    dots_merge_logits(N_CHUNKS - 1)
    xb_ext[0:POOL_TAIL, :] = xb_ext[ROWS:ROWS + POOL_TAIL, :]

    for c in range(N_CHUNKS):
        c0 = c * CW
        pa = _dot(ya[...], plru_ref[:, c0:c0 + CW])
        pb = _dot(yb[...], ppool_ref[:, c0:c0 + CW])
        mbuf[:, c0:c0 + CW] = (_tanh_p1(tb[TB_MA + c]) * pa + _tanh_p1(tb[TB_MB + c]) * pb).astype(BF16)

    for c in range(N_CHUNKS):
        c0 = c * CW
        x1[:, c0:c0 + CW] = x_rows(xs, 0, ROWS, c0, CW) + _dot(mbuf[...], wout_ref[:, c0:c0 + CW])
    norm_in(xs1, nslot)
    start_copies(x_copy, nxt2, xs2)
    start_copies(p_copy, nxt2, xs2)

    for c in range(N_CHUNKS):
        tb[TB_PE + c] = _dot(pbf[...], wpe_ref[:, c * CW:(c + 1) * CW])
    for r0 in range(0, ROWS, NORM_RT):
        hb2[pl.ds(r0, NORM_RT), :] = rmsnorm(x1[pl.ds(r0, NORM_RT), :], pg_ref, BF16)

    for c in range(N_CHUNKS):
        c0 = c * CW
        gate = _dot(hb2[...], wpg_ref[:, c0:c0 + CW])
        x1[:, c0:c0 + CW] = x1[:, c0:c0 + CW] + _tanh_p1(gate) * tb[TB_PE + c]
    dots_xa_ga(nslot, 0)
    dots_xa_ga(nslot, 1)

    for r0 in range(0, ROWS, NORM_RT):
        ov = rmsnorm(x1[pl.ds(r0, NORM_RT), :], fg_ref, F32)
        obuf[slot, pl.ds(r0 // BATCH, NORM_RT // BATCH), :, :] = ov.reshape(NORM_RT // BATCH, BATCH, D_MODEL)

    @pl.when(ti == n_steps - 1)
    def _():
        start_copies(o_copy, ti, slot)
        wait_slot(xbuf, sem_x, xs2)
        wait_slot(pbuf, sem_p, xs1)
        wait_slot(pbuf, sem_p, xs2)
        wait_slot(obuf, sem_o, nslot)
        wait_slot(obuf, sem_o, slot)


def _const_spec(shape):
    zeros = (0,) * len(shape)
    return pl.BlockSpec(shape, lambda i: zeros, pipeline_mode=pl.Buffered(1))


@jax.jit
def kernel(x, p, norm_g, w_in, conv_w, conv_b, lru_w_a, lru_b_a, lru_w_x, lru_b_x, lru_lambda,
           pool_w, pool_scale, w_proj_lru, w_proj_pool, w_out, ple_norm_g, w_ple_gate,
           w_ple_proj, final_g):
    assert x.shape == (BATCH, SEQ, D_MODEL) and p.shape == (1, BATCH, SEQ, P_DIM)
    row = lambda v: v.reshape(1, -1)
    operands = [
        x, p, row(norm_g[0]), w_in, conv_w[0], row(conv_b[0]), lru_w_a, row(lru_b_a[0]), lru_w_x,
        row(lru_b_x[0]), row(lru_lambda[0]), pool_w, row(pool_scale[0]), w_proj_lru, w_proj_pool,
        w_out, row(ple_norm_g[0]), w_ple_gate, w_ple_proj, row(final_g),
    ]
    in_specs = [pl.BlockSpec(memory_space=pl.ANY) if v.ndim > 2 else _const_spec(v.shape)
                for v in operands]
    scratch = [
        pltpu.VMEM((X_SLOTS, TT, BATCH, D_MODEL), F32),
        pltpu.VMEM((X_SLOTS, TT, BATCH, P_DIM), F32),
        pltpu.VMEM((2, TT, BATCH, D_MODEL), F32),
        pltpu.SemaphoreType.DMA((X_SLOTS,)),
        pltpu.SemaphoreType.DMA((X_SLOTS,)),
        pltpu.SemaphoreType.DMA((2,)),
        pltpu.SemaphoreType.DMA((N_TB,)),
        pltpu.VMEM((D_MODEL, IN_COLS), BF16),
        pltpu.VMEM((LRU_HEADS, LRU_HEAD_DIM, 2 * LRU_HEAD_DIM), BF16),
        pltpu.VMEM((POOL_GROUPS, POOL_GROUP_DIM, POOL_GROUP_DIM), BF16),
        pltpu.VMEM((LRU_WIDTH, D_MODEL), BF16),
        pltpu.VMEM((POOL_WIDTH, D_MODEL), BF16),
        pltpu.VMEM((D_MODEL, D_MODEL), BF16),
        pltpu.VMEM((D_MODEL, D_MODEL), BF16),
        pltpu.VMEM((P_DIM, D_MODEL), BF16),
        pltpu.VMEM((2, ROWS, D_MODEL), BF16),
        pltpu.VMEM((ROWS, D_MODEL), BF16),
        pltpu.VMEM((ROWS, P_DIM), BF16),
        pltpu.VMEM((N_CHUNKS, ROWS + CONV_TAIL, CW), F32),
        pltpu.VMEM((ROWS + POOL_TAIL, POOL_WIDTH), F32),
        pltpu.VMEM((BATCH, LRU_WIDTH), F32),
        pltpu.VMEM((N_TB, ROWS, CW), F32),
        pltpu.VMEM((2, ROWS, CW), BF16),
        pltpu.VMEM((ROWS, LRU_WIDTH), BF16),
        pltpu.VMEM((ROWS, POOL_WIDTH), BF16),
        pltpu.VMEM((POOL_GROUPS, ROWS, POOL_GROUP_DIM), BF16),
        pltpu.VMEM((ROWS, D_MODEL), BF16),
        pltpu.VMEM((ROWS, D_MODEL), F32),
    ]
    scratch_vmem_bytes = sum(math.prod(s.shape) * jnp.dtype(s.dtype).itemsize
                             for s in scratch if s.memory_space == pltpu.VMEM)
    return pl.pallas_call(
        _block_kernel,
        grid=(SEQ // TT,),
        in_specs=in_specs,
        out_specs=pl.BlockSpec(memory_space=pl.ANY),
        out_shape=jax.ShapeDtypeStruct((BATCH, SEQ, D_MODEL), F32),
        scratch_shapes=scratch,
        compiler_params=pltpu.CompilerParams(
            dimension_semantics=("arbitrary",),
            vmem_limit_bytes=scratch_vmem_bytes + COMPILER_VMEM_ALLOWANCE_BYTES,
        ),
        name="rglru_pool_block",
    )(*operands)
```

```python
import functools
import math

import jax
import jax.numpy as jnp
from jax import lax
from jax.experimental import pallas as pl
from jax.experimental.pallas import tpu as pltpu

D_MODEL = 1024
BATCH = 16
SEQ = 2048
P_DIM = 256
LRU_WIDTH = 1024
LRU_HEADS = 8
LRU_HEAD_DIM = 128
CONV_WIDTH = 4
LRU_C = 8.0
POOL_WIDTH = 512
POOL_WINDOWS = (2, 4, 8, 16)
POOL_GROUPS = 4
POOL_GROUP_DIM = 128
MAX_WIN = 16
EPS = 1e-6

OFF_XA = 0
OFF_GA = LRU_WIDTH
OFF_XB = 2 * LRU_WIDTH
OFF_GB = OFF_XB + POOL_WIDTH
OFF_MA = OFF_GB + POOL_WIDTH
OFF_MB = OFF_MA + D_MODEL
IN_COLS = OFF_MB + D_MODEL

TT = 32
ROWS = TT * BATCH
CW = 256
LANES = 128
N_CHUNKS = D_MODEL // CW
HEADS_PER_CHUNK = CW // LRU_HEAD_DIM
NORM_RT = 32
CONV_TAIL = (CONV_WIDTH - 1) * BATCH
POOL_TAIL = MAX_WIN * BATCH
X_SLOTS = 3
COMPILER_VMEM_ALLOWANCE_BYTES = 4 * 1024 * 1024

TB_GA = 0
TB_XC, TB_G0, TB_G1, TB_A, TB_U = 4, 6, 8, 10, 12
TB_MA, TB_MB = 14, 18
TB_GB = 22
N_TB = 24
TB_PE = 6

F32 = jnp.float32
BF16 = jnp.bfloat16
TINY = 1e-30


def _tanh_p1(vh):
    return jnp.tanh(vh) + 1.0


def _dot(a, b):
    return jnp.dot(a, b, preferred_element_type=F32)


def _block_kernel(x_hbm, p_hbm, ng_ref, win_hbm, cw_ref, cb_ref, wa_hbm, ba_ref, wx_hbm, bx_ref,
                  lam_ref, pw_hbm, ps_ref, plru_hbm, ppool_hbm, wout_hbm, pg_ref, wpg_hbm,
                  wpe_hbm, fg_ref,
                  o_hbm,
                  xbuf, pbuf, obuf, sem_x, sem_p, sem_o, sem_w,
                  win_ref, wax_ref, pw_ref, plru_ref, ppool_ref, wout_ref, wpg_ref, wpe_ref,
                  hb_in, hb2, pbf, xa_ext, xb_ext, hst, tb, xcb, ya, yb, dbuf, mbuf, x1):
    ti = pl.program_id(0)
    n_steps = pl.num_programs(0)
    slot = ti % 2
    nxt = jnp.minimum(ti + 1, n_steps - 1)
    nxt2 = jnp.minimum(ti + 2, n_steps - 1)
    nslot = 1 - slot
    xs, xs1, xs2 = ti % X_SLOTS, (ti + 1) % X_SLOTS, (ti + 2) % X_SLOTS

    def x_copy(step, sl, b):
        return pltpu.make_async_copy(x_hbm.at[b, pl.ds(step * TT, TT), :], xbuf.at[sl, :, b, :],
                                     sem_x.at[sl])

    def p_copy(step, sl, b):
        return pltpu.make_async_copy(p_hbm.at[0, b, pl.ds(step * TT, TT), :], pbuf.at[sl, :, b, :],
                                     sem_p.at[sl])

    def o_copy(step, sl, b):
        return pltpu.make_async_copy(obuf.at[sl, :, b, :], o_hbm.at[b, pl.ds(step * TT, TT), :],
                                     sem_o.at[sl])

    def start_copies(copy, step, sl):
        for b in range(BATCH):
            copy(step, sl, b).start()

    def wait_slot(buf, sem, sl):
        pltpu.make_async_copy(buf.at[sl], buf.at[sl], sem.at[sl]).wait()

    def x_rows(sl, r0, nrows, c0=0, ncols=D_MODEL):
        t0, nt = r0 // BATCH, nrows // BATCH
        return xbuf[sl, pl.ds(t0, nt), :, c0:c0 + ncols].reshape(nrows, ncols)

    def rmsnorm(xv, g_ref, cast):
        ms = jnp.mean(xv * xv, axis=-1, keepdims=True)
        return ((xv * lax.rsqrt(ms + EPS)) * g_ref[...]).astype(cast)

    def win(off, c):
        return win_ref[:, off + c * CW:off + (c + 1) * CW]

    def norm_in(xsl, sl):
        for r0 in range(0, ROWS, NORM_RT):
            hb_in[sl, pl.ds(r0, NORM_RT), :] = rmsnorm(x_rows(xsl, r0, NORM_RT), ng_ref, BF16)

    def dots_xa_ga(sl, c):
        xa_ext[c, CONV_TAIL:CONV_TAIL + ROWS, :] = _dot(hb_in[sl], win(OFF_XA, c))
        tb[TB_GA + c] = _dot(hb_in[sl], win(OFF_GA, c))

    def weight_tiles():
        tiles = []

        def add(src, dst, nrows, ncols, scale):
            for r0 in range(0, nrows, ROWS):
                nr = min(ROWS, nrows - r0)
                for c0 in range(0, ncols, CW):
                    tiles.append((src.at[0, pl.ds(r0, nr), pl.ds(c0, CW)], nr, CW,
                                  dst.at[pl.ds(r0, nr), pl.ds(c0, CW)],
                                  scale(c0) if callable(scale) else scale))

        halved_in = lambda c0: 0.5 if (OFF_GA <= c0 < OFF_XB or c0 >= OFF_GB) else 1.0
        add(win_hbm, win_ref, D_MODEL, IN_COLS, halved_in)
        add(plru_hbm, plru_ref, LRU_WIDTH, D_MODEL, 0.5)
        add(ppool_hbm, ppool_ref, POOL_WIDTH, D_MODEL, 0.5)
        add(wout_hbm, wout_ref, D_MODEL, D_MODEL, 1.0)
        add(wpg_hbm, wpg_ref, D_MODEL, D_MODEL, 0.5)
        add(wpe_hbm, wpe_ref, P_DIM, D_MODEL, 0.5)
        for h in range(LRU_HEADS):
            for part, src in enumerate((wa_hbm, wx_hbm)):
                tiles.append((src.at[0, h], LRU_HEAD_DIM, LRU_HEAD_DIM,
                              wax_ref.at[h, :, pl.ds(part * LRU_HEAD_DIM, LRU_HEAD_DIM)], 0.5))
        for g in range(POOL_GROUPS):
            tiles.append((pw_hbm.at[0, g], POOL_GROUP_DIM, POOL_GROUP_DIM, pw_ref.at[g], 1.0))
        return tiles

    def load_weights():
        tiles = weight_tiles()
        copies = [pltpu.make_async_copy(src, tb.at[i % N_TB, pl.ds(0, nr), pl.ds(0, nc)], sem_w.at[i % N_TB])
                  for i, (src, nr, nc, _, _) in enumerate(tiles)]
        for i in range(len(tiles) + N_TB):
            done = i - N_TB
            if done >= 0:
                _, nr, nc, dst, scale = tiles[done]
                copies[done].wait()
                v = tb[done % N_TB, 0:nr, 0:nc]
                dst[...] = (v if scale == 1.0 else v * scale).astype(BF16)
            if i < len(tiles):
                copies[i].start()

    @pl.when(ti == 0)
    def _():
        start_copies(x_copy, 0, 0)
        start_copies(x_copy, 1, 1)
        start_copies(p_copy, 0, 0)
        start_copies(p_copy, 1, 1)
        load_weights()
        xa_ext[:, 0:CONV_TAIL, :] = jnp.zeros((N_CHUNKS, CONV_TAIL, CW), F32)
        xb_ext[0:POOL_TAIL, :] = jnp.zeros((POOL_TAIL, POOL_WIDTH), F32)
        hst[...] = jnp.zeros((BATCH, LRU_WIDTH), F32)
        wait_slot(xbuf, sem_x, 0)
        norm_in(0, 0)
        dots_xa_ga(0, 0)
        dots_xa_ga(0, 1)

    @pl.when(ti >= 1)
    def _():
        start_copies(o_copy, ti - 1, nslot)

    @pl.when(ti >= 2)
    def _():
        wait_slot(obuf, sem_o, slot)

    wait_slot(pbuf, sem_p, xs)
    wait_slot(xbuf, sem_x, xs1)

    hb = hb_in.at[slot]
    pbf[...] = pbuf[xs].reshape(ROWS, P_DIM).astype(BF16)

    lam = lam_ref[...]
    half_c = (-0.5 * LRU_C) * (jnp.maximum(-lam, 0.0) + jnp.log1p(jnp.exp(-jnp.abs(lam))))

    def conv(c):
        q, c0 = c % 2, c * CW
        acc = cb_ref[:, c0:c0 + CW]
        for k in range(CONV_WIDTH):
            acc = acc + xa_ext[c, k * BATCH:k * BATCH + ROWS, :] * cw_ref[k:k + 1, c0:c0 + CW]
        tb[TB_XC + q] = acc
        xcb[q] = acc.astype(BF16)
        xa_ext[c, 0:CONV_TAIL, :] = xa_ext[c, ROWS:ROWS + CONV_TAIL, :]

    def dots_gates(c):
        q = c % 2
        for hh, tg in enumerate((TB_G0, TB_G1)):
            l0 = hh * LRU_HEAD_DIM
            tb[tg + q] = _dot(xcb[q, :, l0:l0 + LRU_HEAD_DIM], wax_ref[c * HEADS_PER_CHUNK + hh])

    def gates(c):
        q = c % 2
        for hh, tg in enumerate((TB_G0, TB_G1)):
            l0 = hh * LRU_HEAD_DIM
            ch = slice(c * CW + l0, c * CW + l0 + LRU_HEAD_DIM)
            lanes = slice(l0, l0 + LRU_HEAD_DIM)
            r_tanh = jnp.tanh(tb[tg + q, :, 0:LRU_HEAD_DIM] + 0.5 * ba_ref[:, ch])
            i_gate = 0.5 * jnp.tanh(tb[tg + q, :, LRU_HEAD_DIM:2 * LRU_HEAD_DIM] + 0.5 * bx_ref[:, ch]) + 0.5
            a = jnp.exp(half_c[:, ch] * r_tanh + half_c[:, ch])
            y = 1.0 - a * a
            mult = y * lax.rsqrt(jnp.maximum(y, TINY))
            tb[TB_A + q, :, lanes] = a
            tb[TB_U + q, :, lanes] = mult * (i_gate * tb[TB_XC + q, :, lanes])

    def scan(c):
        q, c0 = c % 2, c * CW
        h = hst[:, c0:c0 + CW]
        for t in range(TT):
            rows = pl.ds(t * BATCH, BATCH)
            h = tb[TB_A + q, rows, :] * h + tb[TB_U + q, rows, :]
            ga = tb[TB_GA + c, rows, :]
            ya[rows, c0:c0 + CW] = (h * (_tanh_p1(ga) * ga)).astype(BF16)
        hst[:, c0:c0 + CW] = h

    def dots_pool_in():
        xb_ext[POOL_TAIL:POOL_TAIL + ROWS, :] = _dot(hb[...], win_ref[:, OFF_XB:OFF_XB + POOL_WIDTH])
        for half in range(POOL_WIDTH // CW):
            tb[TB_GB + half] = _dot(hb[...], win(OFF_GB, half))

    def pool_windows(g):
        k = POOL_WINDOWS[g]
        lanes = slice(g * POOL_GROUP_DIM, (g + 1) * POOL_GROUP_DIM)
        cur = xb_ext[POOL_TAIL:POOL_TAIL + ROWS, lanes]
        s = cur
        for j in range(1, k):
            s = s + xb_ext[POOL_TAIL - j * BATCH:POOL_TAIL - j * BATCH + ROWS, lanes]
        row = lax.broadcasted_iota(jnp.int32, (ROWS, POOL_GROUP_DIM), 0)
        pos = ti * TT + row // BATCH
        cnt = jnp.minimum(pos + 1, k).astype(F32)
        dbuf[g] = (s / cnt - cur).astype(BF16)

    def pool_out(g):
        lanes = slice(g * POOL_GROUP_DIM, (g + 1) * POOL_GROUP_DIM)
        half, gg = divmod(g, CW // POOL_GROUP_DIM)
        gb = tb[TB_GB + half, :, gg * POOL_GROUP_DIM:(gg + 1) * POOL_GROUP_DIM]
        y = _dot(dbuf[g], pw_ref[g]) * ps_ref[:, lanes]
        yb[:, lanes] = (y * (_tanh_p1(gb) * gb)).astype(BF16)

    def dots_merge_logits(c):
        tb[TB_MA + c] = _dot(hb[...], win(OFF_MA, c))
        tb[TB_MB + c] = _dot(hb[...], win(OFF_MB, c))

    for c in range(N_CHUNKS):
        conv(c)
        dots_gates(c)
        if c + 2 < N_CHUNKS:
            dots_xa_ga(slot, c + 2)
        if c == 0:
            dots_pool_in()
        else:
            dots_merge_logits(c - 1)
        gates(c)
        scan(c)
        pool_windows(c)
        pool_out(c)
    dots_merge_logits(N_CHUNKS - 1)
    xb_ext[0:POOL_TAIL, :] = xb_ext[ROWS:ROWS + POOL_TAIL, :]

    for c in range(N_CHUNKS):
        c0 = c * CW
        pa = _dot(ya[...], plru_ref[:, c0:c0 + CW])
        pb = _dot(yb[...], ppool_ref[:, c0:c0 + CW])
        mbuf[:, c0:c0 + CW] = (_tanh_p1(tb[TB_MA + c]) * pa + _tanh_p1(tb[TB_MB + c]) * pb).astype(BF16)

    def lane_partial_sq(v):
        tiles = [v[:, l0:l0 + LANES] for l0 in range(0, CW, LANES)]
        return functools.reduce(lambda a, b: a + b, [t * t for t in tiles])

    def inv_rms(partials):
        total = functools.reduce(lambda a, b: a + b, partials)
        return lax.rsqrt(jnp.sum(total, axis=-1, keepdims=True) * (1.0 / D_MODEL) + EPS)

    sq1 = []
    for c in range(N_CHUNKS):
        c0 = c * CW
        x1c = x_rows(xs, 0, ROWS, c0, CW) + _dot(mbuf[...], wout_ref[:, c0:c0 + CW])
        x1[:, c0:c0 + CW] = x1c
        sq1.append(lane_partial_sq(x1c))
    norm_in(xs1, nslot)
    start_copies(x_copy, nxt2, xs2)
    start_copies(p_copy, nxt2, xs2)

    for c in range(N_CHUNKS):
        tb[TB_PE + c] = _dot(pbf[...], wpe_ref[:, c * CW:(c + 1) * CW])
    hb2[...] = ((x1[...] * inv_rms(sq1)) * pg_ref[...]).astype(BF16)

    sq2 = []
    for c in range(N_CHUNKS):
        c0 = c * CW
        gate = _dot(hb2[...], wpg_ref[:, c0:c0 + CW])
        x2c = x1[:, c0:c0 + CW] + _tanh_p1(gate) * tb[TB_PE + c]
        x1[:, c0:c0 + CW] = x2c
        sq2.append(lane_partial_sq(x2c))
    dots_xa_ga(nslot, 0)
    dots_xa_ga(nslot, 1)

    ov = (x1[...] * inv_rms(sq2)) * fg_ref[...]
    obuf[slot] = ov.reshape(TT, BATCH, D_MODEL)

    @pl.when(ti == n_steps - 1)
    def _():
        start_copies(o_copy, ti, slot)
        wait_slot(xbuf, sem_x, xs2)
        wait_slot(pbuf, sem_p, xs1)
        wait_slot(pbuf, sem_p, xs2)
        wait_slot(obuf, sem_o, nslot)
        wait_slot(obuf, sem_o, slot)


def _const_spec(shape):
    zeros = (0,) * len(shape)
    return pl.BlockSpec(shape, lambda i: zeros, pipeline_mode=pl.Buffered(1))


@jax.jit
def kernel(x, p, norm_g, w_in, conv_w, conv_b, lru_w_a, lru_b_a, lru_w_x, lru_b_x, lru_lambda,
           pool_w, pool_scale, w_proj_lru, w_proj_pool, w_out, ple_norm_g, w_ple_gate,
           w_ple_proj, final_g):
    assert x.shape == (BATCH, SEQ, D_MODEL) and p.shape == (1, BATCH, SEQ, P_DIM)
    row = lambda v: v.reshape(1, -1)
    operands = [
        x, p, row(norm_g[0]), w_in, conv_w[0], row(conv_b[0]), lru_w_a, row(lru_b_a[0]), lru_w_x,
        row(lru_b_x[0]), row(lru_lambda[0]), pool_w, row(pool_scale[0]), w_proj_lru, w_proj_pool,
        w_out, row(ple_norm_g[0]), w_ple_gate, w_ple_proj, row(final_g),
    ]
    in_specs = [pl.BlockSpec(memory_space=pl.ANY) if v.ndim > 2 else _const_spec(v.shape)
                for v in operands]
    scratch = [
        pltpu.VMEM((X_SLOTS, TT, BATCH, D_MODEL), F32),
        pltpu.VMEM((X_SLOTS, TT, BATCH, P_DIM), F32),
        pltpu.VMEM((2, TT, BATCH, D_MODEL), F32),
        pltpu.SemaphoreType.DMA((X_SLOTS,)),
        pltpu.SemaphoreType.DMA((X_SLOTS,)),
        pltpu.SemaphoreType.DMA((2,)),
        pltpu.SemaphoreType.DMA((N_TB,)),
        pltpu.VMEM((D_MODEL, IN_COLS), BF16),
        pltpu.VMEM((LRU_HEADS, LRU_HEAD_DIM, 2 * LRU_HEAD_DIM), BF16),
        pltpu.VMEM((POOL_GROUPS, POOL_GROUP_DIM, POOL_GROUP_DIM), BF16),
        pltpu.VMEM((LRU_WIDTH, D_MODEL), BF16),
        pltpu.VMEM((POOL_WIDTH, D_MODEL), BF16),
        pltpu.VMEM((D_MODEL, D_MODEL), BF16),
        pltpu.VMEM((D_MODEL, D_MODEL), BF16),
        pltpu.VMEM((P_DIM, D_MODEL), BF16),
        pltpu.VMEM((2, ROWS, D_MODEL), BF16),
        pltpu.VMEM((ROWS, D_MODEL), BF16),
        pltpu.VMEM((ROWS, P_DIM), BF16),
        pltpu.VMEM((N_CHUNKS, ROWS + CONV_TAIL, CW), F32),
        pltpu.VMEM((ROWS + POOL_TAIL, POOL_WIDTH), F32),
        pltpu.VMEM((BATCH, LRU_WIDTH), F32),
        pltpu.VMEM((N_TB, ROWS, CW), F32),
        pltpu.VMEM((2, ROWS, CW), BF16),
        pltpu.VMEM((ROWS, LRU_WIDTH), BF16),
        pltpu.VMEM((ROWS, POOL_WIDTH), BF16),
        pltpu.VMEM((POOL_GROUPS, ROWS, POOL_GROUP_DIM), BF16),
        pltpu.VMEM((ROWS, D_MODEL), BF16),
        pltpu.VMEM((ROWS, D_MODEL), F32),
    ]
    scratch_vmem_bytes = sum(math.prod(s.shape) * jnp.dtype(s.dtype).itemsize
                             for s in scratch if s.memory_space == pltpu.VMEM)
    return pl.pallas_call(
        _block_kernel,
        grid=(SEQ // TT,),
        in_specs=in_specs,
        out_specs=pl.BlockSpec(memory_space=pl.ANY),
        out_shape=jax.ShapeDtypeStruct((BATCH, SEQ, D_MODEL), F32),
        scratch_shapes=scratch,
        compiler_params=pltpu.CompilerParams(
            dimension_semantics=("arbitrary",),
            vmem_limit_bytes=scratch_vmem_bytes + COMPILER_VMEM_ALLOWANCE_BYTES,
        ),
        name="rglru_pool_block",
    )(*operands)
```

```python
import math

import jax
import jax.numpy as jnp
from jax import lax
from jax.experimental import pallas as pl
from jax.experimental.pallas import tpu as pltpu

D_MODEL = 1024
BATCH = 16
SEQ = 2048
P_DIM = 256
LRU_WIDTH = 1024
LRU_HEADS = 8
LRU_HEAD_DIM = 128
CONV_WIDTH = 4
LRU_C = 8.0
POOL_WIDTH = 512
POOL_WINDOWS = (2, 4, 8, 16)
POOL_GROUPS = 4
POOL_GROUP_DIM = 128
MAX_WIN = 16
EPS = 1e-6

OFF_XA = 0
OFF_GA = LRU_WIDTH
OFF_XB = 2 * LRU_WIDTH
OFF_GB = OFF_XB + POOL_WIDTH
OFF_MA = OFF_GB + POOL_WIDTH
OFF_MB = OFF_MA + D_MODEL
IN_COLS = OFF_MB + D_MODEL

TT = 32
ROWS = TT * BATCH
CW = 256
N_CHUNKS = D_MODEL // CW
HEADS_PER_CHUNK = CW // LRU_HEAD_DIM
NORM_RT = 32
CONV_TAIL = (CONV_WIDTH - 1) * BATCH
POOL_TAIL = MAX_WIN * BATCH
X_SLOTS = 3
COMPILER_VMEM_ALLOWANCE_BYTES = 4 * 1024 * 1024

TB_GA = 0
TB_XC, TB_G0, TB_G1, TB_A, TB_U = 4, 6, 8, 10, 12
TB_MA, TB_MB = 14, 18
TB_GB = 22
N_TB = 24
TB_PE = 6

F32 = jnp.float32
BF16 = jnp.bfloat16
TINY = 1e-30


def _tanh_p1(vh):
    return jnp.tanh(vh) + 1.0


def _dot(a, b):
    return jnp.dot(a, b, preferred_element_type=F32)


def _block_kernel(x_hbm, p_hbm, ng_ref, win_hbm, cw_ref, cb_ref, wa_hbm, ba_ref, wx_hbm, bx_ref,
                  lam_ref, pw_hbm, ps_ref, plru_hbm, ppool_hbm, wout_hbm, pg_ref, wpg_hbm,
                  wpe_hbm, fg_ref,
                  o_hbm,
                  xbuf, pbuf, obuf, sem_x, sem_p, sem_o, sem_w,
                  win_ref, wax_ref, pw_ref, plru_ref, ppool_ref, wout_ref, wpg_ref, wpe_ref,
                  hb_in, hb2, pbf, xa_ext, xb_ext, hst, tb, xcb, ya, yb, dbuf, mbuf, x1):
    ti = pl.program_id(0)
    n_steps = pl.num_programs(0)
    slot = ti % 2
    nxt = jnp.minimum(ti + 1, n_steps - 1)
    nxt2 = jnp.minimum(ti + 2, n_steps - 1)
    nslot = 1 - slot
    xs, xs1, xs2 = ti % X_SLOTS, (ti + 1) % X_SLOTS, (ti + 2) % X_SLOTS

    def x_copy(step, sl, b):
        return pltpu.make_async_copy(x_hbm.at[b, pl.ds(step * TT, TT), :], xbuf.at[sl, :, b, :],
                                     sem_x.at[sl])

    def p_copy(step, sl, b):
        return pltpu.make_async_copy(p_hbm.at[0, b, pl.ds(step * TT, TT), :], pbuf.at[sl, :, b, :],
                                     sem_p.at[sl])

    def o_copy(step, sl, b):
        return pltpu.make_async_copy(obuf.at[sl, :, b, :], o_hbm.at[b, pl.ds(step * TT, TT), :],
                                     sem_o.at[sl])

    def start_copies(copy, step, sl):
        for b in range(BATCH):
            copy(step, sl, b).start()

    def wait_slot(buf, sem, sl):
        pltpu.make_async_copy(buf.at[sl], buf.at[sl], sem.at[sl]).wait()

    def x_rows(sl, r0, nrows, c0=0, ncols=D_MODEL):
        t0, nt = r0 // BATCH, nrows // BATCH
        return xbuf[sl, pl.ds(t0, nt), :, c0:c0 + ncols].reshape(nrows, ncols)

    def rmsnorm(xv, g_ref, cast):
        ms = jnp.mean(xv * xv, axis=-1, keepdims=True)
        return ((xv * lax.rsqrt(ms + EPS)) * g_ref[...]).astype(cast)

    def win(off, c):
        return win_ref[:, off + c * CW:off + (c + 1) * CW]

    def norm_in(xsl, sl):
        for r0 in range(0, ROWS, NORM_RT):
            hb_in[sl, pl.ds(r0, NORM_RT), :] = rmsnorm(x_rows(xsl, r0, NORM_RT), ng_ref, BF16)

    def dots_xa_ga(sl, c):
        xa_ext[c, CONV_TAIL:CONV_TAIL + ROWS, :] = _dot(hb_in[sl], win(OFF_XA, c))
        tb[TB_GA + c] = _dot(hb_in[sl], win(OFF_GA, c))

    def weight_tiles():
        tiles = []

        def add(src, dst, nrows, ncols, scale):
            for r0 in range(0, nrows, ROWS):
                nr = min(ROWS, nrows - r0)
                for c0 in range(0, ncols, CW):
                    tiles.append((src.at[0, pl.ds(r0, nr), pl.ds(c0, CW)], nr, CW,
                                  dst.at[pl.ds(r0, nr), pl.ds(c0, CW)],
                                  scale(c0) if callable(scale) else scale))

        halved_in = lambda c0: 0.5 if (OFF_GA <= c0 < OFF_XB or c0 >= OFF_GB) else 1.0
        add(win_hbm, win_ref, D_MODEL, IN_COLS, halved_in)
        add(plru_hbm, plru_ref, LRU_WIDTH, D_MODEL, 0.5)
        add(ppool_hbm, ppool_ref, POOL_WIDTH, D_MODEL, 0.5)
        add(wout_hbm, wout_ref, D_MODEL, D_MODEL, 1.0)
        add(wpg_hbm, wpg_ref, D_MODEL, D_MODEL, 0.5)
        add(wpe_hbm, wpe_ref, P_DIM, D_MODEL, 0.5)
        for h in range(LRU_HEADS):
            for part, src in enumerate((wa_hbm, wx_hbm)):
                tiles.append((src.at[0, h], LRU_HEAD_DIM, LRU_HEAD_DIM,
                              wax_ref.at[h, :, pl.ds(part * LRU_HEAD_DIM, LRU_HEAD_DIM)], 0.5))
        for g in range(POOL_GROUPS):
            tiles.append((pw_hbm.at[0, g], POOL_GROUP_DIM, POOL_GROUP_DIM, pw_ref.at[g], 1.0))
        return tiles

    def load_weights():
        tiles = weight_tiles()
        copies = [pltpu.make_async_copy(src, tb.at[i % N_TB, pl.ds(0, nr), pl.ds(0, nc)], sem_w.at[i % N_TB])
                  for i, (src, nr, nc, _, _) in enumerate(tiles)]
        for i in range(len(tiles) + N_TB):
            done = i - N_TB
            if done >= 0:
                _, nr, nc, dst, scale = tiles[done]
                copies[done].wait()
                v = tb[done % N_TB, 0:nr, 0:nc]
                dst[...] = (v if scale == 1.0 else v * scale).astype(BF16)
            if i < len(tiles):
                copies[i].start()

    @pl.when(ti == 0)
    def _():
        start_copies(x_copy, 0, 0)
        start_copies(x_copy, 1, 1)
        start_copies(p_copy, 0, 0)
        start_copies(p_copy, 1, 1)
        load_weights()
        xa_ext[:, 0:CONV_TAIL, :] = jnp.zeros((N_CHUNKS, CONV_TAIL, CW), F32)
        xb_ext[0:POOL_TAIL, :] = jnp.zeros((POOL_TAIL, POOL_WIDTH), F32)
        hst[...] = jnp.zeros((BATCH, LRU_WIDTH), F32)
        wait_slot(xbuf, sem_x, 0)
        norm_in(0, 0)
        dots_xa_ga(0, 0)
        dots_xa_ga(0, 1)

    @pl.when(ti >= 1)
    def _():
        start_copies(o_copy, ti - 1, nslot)

    @pl.when(ti >= 2)
    def _():
        wait_slot(obuf, sem_o, slot)

    wait_slot(pbuf, sem_p, xs)
    wait_slot(xbuf, sem_x, xs1)

    hb = hb_in.at[slot]
    pbf[...] = pbuf[xs].reshape(ROWS, P_DIM).astype(BF16)

    lam = lam_ref[...]
    half_c = (-0.5 * LRU_C) * (jnp.maximum(-lam, 0.0) + jnp.log1p(jnp.exp(-jnp.abs(lam))))

    def conv(c):
        q, c0 = c % 2, c * CW
        acc = cb_ref[:, c0:c0 + CW]
        for k in range(CONV_WIDTH):
            acc = acc + xa_ext[c, k * BATCH:k * BATCH + ROWS, :] * cw_ref[k:k + 1, c0:c0 + CW]
        tb[TB_XC + q] = acc
        xcb[q] = acc.astype(BF16)
        xa_ext[c, 0:CONV_TAIL, :] = xa_ext[c, ROWS:ROWS + CONV_TAIL, :]

    def dots_gates(c):
        q = c % 2
        for hh, tg in enumerate((TB_G0, TB_G1)):
            l0 = hh * LRU_HEAD_DIM
            tb[tg + q] = _dot(xcb[q, :, l0:l0 + LRU_HEAD_DIM], wax_ref[c * HEADS_PER_CHUNK + hh])

    def gates(c):
        q = c % 2
        for hh, tg in enumerate((TB_G0, TB_G1)):
            l0 = hh * LRU_HEAD_DIM
            ch = slice(c * CW + l0, c * CW + l0 + LRU_HEAD_DIM)
            lanes = slice(l0, l0 + LRU_HEAD_DIM)
            r_tanh = jnp.tanh(tb[tg + q, :, 0:LRU_HEAD_DIM] + 0.5 * ba_ref[:, ch])
            i_gate = 0.5 * jnp.tanh(tb[tg + q, :, LRU_HEAD_DIM:2 * LRU_HEAD_DIM] + 0.5 * bx_ref[:, ch]) + 0.5
            a = jnp.exp(half_c[:, ch] * (r_tanh + 1.0))
            y = 1.0 - a * a
            mult = y * lax.rsqrt(jnp.maximum(y, TINY))
            tb[TB_A + q, :, lanes] = a
            tb[TB_U + q, :, lanes] = mult * (i_gate * tb[TB_XC + q, :, lanes])

    def scan(c):
        q, c0 = c % 2, c * CW
        h = hst[:, c0:c0 + CW]
        for t in range(TT):
            rows = pl.ds(t * BATCH, BATCH)
            h = tb[TB_A + q, rows, :] * h + tb[TB_U + q, rows, :]
            ga = tb[TB_GA + c, rows, :]
            ya[rows, c0:c0 + CW] = (h * (_tanh_p1(ga) * ga)).astype(BF16)
        hst[:, c0:c0 + CW] = h

    def dots_pool_in():
        xb_ext[POOL_TAIL:POOL_TAIL + ROWS, :] = _dot(hb[...], win_ref[:, OFF_XB:OFF_XB + POOL_WIDTH])
        for half in range(POOL_WIDTH // CW):
            tb[TB_GB + half] = _dot(hb[...], win(OFF_GB, half))

    def pool_windows(g):
        k = POOL_WINDOWS[g]
        lanes = slice(g * POOL_GROUP_DIM, (g + 1) * POOL_GROUP_DIM)
        cur = xb_ext[POOL_TAIL:POOL_TAIL + ROWS, lanes]
        s = cur
        for j in range(1, k):
            s = s + xb_ext[POOL_TAIL - j * BATCH:POOL_TAIL - j * BATCH + ROWS, lanes]
        row = lax.broadcasted_iota(jnp.int32, (ROWS, POOL_GROUP_DIM), 0)
        pos = ti * TT + row // BATCH
        cnt = jnp.minimum(pos + 1, k).astype(F32)
        dbuf[g] = (s / cnt - cur).astype(BF16)

    def pool_out(g):
        lanes = slice(g * POOL_GROUP_DIM, (g + 1) * POOL_GROUP_DIM)
        half, gg = divmod(g, CW // POOL_GROUP_DIM)
        gb = tb[TB_GB + half, :, gg * POOL_GROUP_DIM:(gg + 1) * POOL_GROUP_DIM]
        y = _dot(dbuf[g], pw_ref[g]) * ps_ref[:, lanes]
        yb[:, lanes] = (y * (_tanh_p1(gb) * gb)).astype(BF16)

    def dots_merge_logits(c):
        tb[TB_MA + c] = _dot(hb[...], win(OFF_MA, c))
        tb[TB_MB + c] = _dot(hb[...], win(OFF_MB, c))

    for c in range(N_CHUNKS):
        conv(c)
        dots_gates(c)
        if c + 2 < N_CHUNKS:
            dots_xa_ga(slot, c + 2)
        if c == 0:
            dots_pool_in()
        else:
            dots_merge_logits(c - 1)
        gates(c)
        scan(c)
        pool_windows(c)
        pool_out(c)
    dots_merge_logits(N_CHUNKS - 1)
    xb_ext[0:POOL_TAIL, :] = xb_ext[ROWS:ROWS + POOL_TAIL, :]

    for c in range(N_CHUNKS):
        c0 = c * CW
        pa = _dot(ya[...], plru_ref[:, c0:c0 + CW])
        pb = _dot(yb[...], ppool_ref[:, c0:c0 + CW])
        mbuf[:, c0:c0 + CW] = (_tanh_p1(tb[TB_MA + c]) * pa + _tanh_p1(tb[TB_MB + c]) * pb).astype(BF16)

    for c in range(N_CHUNKS):
        c0 = c * CW
        x1[:, c0:c0 + CW] = x_rows(xs, 0, ROWS, c0, CW) + _dot(mbuf[...], wout_ref[:, c0:c0 + CW])
    norm_in(xs1, nslot)
    start_copies(x_copy, nxt2, xs2)
    start_copies(p_copy, nxt2, xs2)

    for c in range(N_CHUNKS):
        tb[TB_PE + c] = _dot(pbf[...], wpe_ref[:, c * CW:(c + 1) * CW])
    for r0 in range(0, ROWS, NORM_RT):
        hb2[pl.ds(r0, NORM_RT), :] = rmsnorm(x1[pl.ds(r0, NORM_RT), :], pg_ref, BF16)

    for c in range(N_CHUNKS):
        c0 = c * CW
        gate = _dot(hb2[...], wpg_ref[:, c0:c0 + CW])
        x1[:, c0:c0 + CW] = x1[:, c0:c0 + CW] + _tanh_p1(gate) * tb[TB_PE + c]
    dots_xa_ga(nslot, 0)
    dots_xa_ga(nslot, 1)

    for r0 in range(0, ROWS, NORM_RT):
        ov = rmsnorm(x1[pl.ds(r0, NORM_RT), :], fg_ref, F32)
        obuf[slot, pl.ds(r0 // BATCH, NORM_RT // BATCH), :, :] = ov.reshape(NORM_RT // BATCH, BATCH, D_MODEL)

    @pl.when(ti == n_steps - 1)
    def _():
        start_copies(o_copy, ti, slot)
        wait_slot(xbuf, sem_x, xs2)
        wait_slot(pbuf, sem_p, xs1)
        wait_slot(pbuf, sem_p, xs2)
        wait_slot(obuf, sem_o, nslot)
        wait_slot(obuf, sem_o, slot)


def _const_spec(shape):
    zeros = (0,) * len(shape)
    return pl.BlockSpec(shape, lambda i: zeros, pipeline_mode=pl.Buffered(1))


@jax.jit
def kernel(x, p, norm_g, w_in, conv_w, conv_b, lru_w_a, lru_b_a, lru_w_x, lru_b_x, lru_lambda,
           pool_w, pool_scale, w_proj_lru, w_proj_pool, w_out, ple_norm_g, w_ple_gate,
           w_ple_proj, final_g):
    assert x.shape == (BATCH, SEQ, D_MODEL) and p.shape == (1, BATCH, SEQ, P_DIM)
    row = lambda v: v.reshape(1, -1)
    operands = [
        x, p, row(norm_g[0]), w_in, conv_w[0], row(conv_b[0]), lru_w_a, row(lru_b_a[0]), lru_w_x,
        row(lru_b_x[0]), row(lru_lambda[0]), pool_w, row(pool_scale[0]), w_proj_lru, w_proj_pool,
        w_out, row(ple_norm_g[0]), w_ple_gate, w_ple_proj, row(final_g),
    ]
    in_specs = [pl.BlockSpec(memory_space=pl.ANY) if v.ndim > 2 else _const_spec(v.shape)
                for v in operands]
    scratch = [
        pltpu.VMEM((X_SLOTS, TT, BATCH, D_MODEL), F32),
        pltpu.VMEM((X_SLOTS, TT, BATCH, P_DIM), F32),
        pltpu.VMEM((2, TT, BATCH, D_MODEL), F32),
        pltpu.SemaphoreType.DMA((X_SLOTS,)),
        pltpu.SemaphoreType.DMA((X_SLOTS,)),
        pltpu.SemaphoreType.DMA((2,)),
        pltpu.SemaphoreType.DMA((N_TB,)),
        pltpu.VMEM((D_MODEL, IN_COLS), BF16),
        pltpu.VMEM((LRU_HEADS, LRU_HEAD_DIM, 2 * LRU_HEAD_DIM), BF16),
        pltpu.VMEM((POOL_GROUPS, POOL_GROUP_DIM, POOL_GROUP_DIM), BF16),
        pltpu.VMEM((LRU_WIDTH, D_MODEL), BF16),
        pltpu.VMEM((POOL_WIDTH, D_MODEL), BF16),
        pltpu.VMEM((D_MODEL, D_MODEL), BF16),
        pltpu.VMEM((D_MODEL, D_MODEL), BF16),
        pltpu.VMEM((P_DIM, D_MODEL), BF16),
        pltpu.VMEM((2, ROWS, D_MODEL), BF16),
        pltpu.VMEM((ROWS, D_MODEL), BF16),
        pltpu.VMEM((ROWS, P_DIM), BF16),
        pltpu.VMEM((N_CHUNKS, ROWS + CONV_TAIL, CW), F32),
        pltpu.VMEM((ROWS + POOL_TAIL, POOL_WIDTH), F32),
        pltpu.VMEM((BATCH, LRU_WIDTH), F32),
        pltpu.VMEM((N_TB, ROWS, CW), F32),
        pltpu.VMEM((2, ROWS, CW), BF16),
        pltpu.VMEM((ROWS, LRU_WIDTH), BF16),
        pltpu.VMEM((ROWS, POOL_WIDTH), BF16),
        pltpu.VMEM((POOL_GROUPS, ROWS, POOL_GROUP_DIM), BF16),
        pltpu.VMEM((ROWS, D_MODEL), BF16),
        pltpu.VMEM((ROWS, D_MODEL), F32),
    ]
    scratch_vmem_bytes = sum(math.prod(s.shape) * jnp.dtype(s.dtype).itemsize
                             for s in scratch if s.memory_space == pltpu.VMEM)
    return pl.pallas_call(
        _block_kernel,
        grid=(SEQ // TT,),
        in_specs=in_specs,
        out_specs=pl.BlockSpec(memory_space=pl.ANY),
        out_shape=jax.ShapeDtypeStruct((BATCH, SEQ, D_MODEL), F32),
        scratch_shapes=scratch,
        compiler_params=pltpu.CompilerParams(
            dimension_semantics=("arbitrary",),
            vmem_limit_bytes=scratch_vmem_bytes + COMPILER_VMEM_ALLOWANCE_BYTES,
        ),
        name="rglru_pool_block",
    )(*operands)
```

```python
import math

import jax
import jax.numpy as jnp
from jax import lax
from jax.experimental import pallas as pl
from jax.experimental.pallas import tpu as pltpu

D_MODEL = 1024
BATCH = 16
SEQ = 2048
P_DIM = 256
LRU_WIDTH = 1024
LRU_HEADS = 8
LRU_HEAD_DIM = 128
CONV_WIDTH = 4
LRU_C = 8.0
POOL_WIDTH = 512
POOL_WINDOWS = (2, 4, 8, 16)
POOL_GROUPS = 4
POOL_GROUP_DIM = 128
MAX_WIN = 16
EPS = 1e-6

OFF_XA = 0
OFF_GA = LRU_WIDTH
OFF_XB = 2 * LRU_WIDTH
OFF_GB = OFF_XB + POOL_WIDTH
OFF_MA = OFF_GB + POOL_WIDTH
OFF_MB = OFF_MA + D_MODEL
IN_COLS = OFF_MB + D_MODEL

TT = 32
ROWS = TT * BATCH
CW = 256
N_CHUNKS = D_MODEL // CW
HEADS_PER_CHUNK = CW // LRU_HEAD_DIM
NORM_RT = 32
CONV_TAIL = (CONV_WIDTH - 1) * BATCH
POOL_TAIL = MAX_WIN * BATCH
X_SLOTS = 3
COMPILER_VMEM_ALLOWANCE_BYTES = 4 * 1024 * 1024

TB_GA = 0
TB_G0, TB_G1, TB_A, TB_U = 6, 8, 10, 12
TB_MA, TB_MB = 14, 18
TB_GB = 22
N_TB = 24
TB_PE = 6

F32 = jnp.float32
BF16 = jnp.bfloat16
TINY = 1e-30


def _tanh_p1(vh):
    return jnp.tanh(vh) + 1.0


def _dot(a, b):
    return jnp.dot(a, b, preferred_element_type=F32)


def _block_kernel(x_hbm, p_hbm, ng_ref, win_hbm, cw_ref, cb_ref, wa_hbm, ba_ref, wx_hbm, bx_ref,
                  lam_ref, pw_hbm, ps_ref, plru_hbm, ppool_hbm, wout_hbm, pg_ref, wpg_hbm,
                  wpe_hbm, fg_ref,
                  o_hbm,
                  xbuf, pbuf, obuf, sem_x, sem_p, sem_o, sem_w,
                  win_ref, wax_ref, pw_ref, plru_ref, ppool_ref, wout_ref, wpg_ref, wpe_ref,
                  hb_in, hb2, pbf, xa_ext, xb_ext, hst, tb, xcb, ya, yb, dbuf, mbuf, x1):
    ti = pl.program_id(0)
    n_steps = pl.num_programs(0)
    slot = ti % 2
    nxt = jnp.minimum(ti + 1, n_steps - 1)
    nxt2 = jnp.minimum(ti + 2, n_steps - 1)
    nslot = 1 - slot
    xs, xs1, xs2 = ti % X_SLOTS, (ti + 1) % X_SLOTS, (ti + 2) % X_SLOTS

    def x_copy(step, sl, b):
        return pltpu.make_async_copy(x_hbm.at[b, pl.ds(step * TT, TT), :], xbuf.at[sl, :, b, :],
                                     sem_x.at[sl])

    def p_copy(step, sl, b):
        return pltpu.make_async_copy(p_hbm.at[0, b, pl.ds(step * TT, TT), :], pbuf.at[sl, :, b, :],
                                     sem_p.at[sl])

    def o_copy(step, sl, b):
        return pltpu.make_async_copy(obuf.at[sl, :, b, :], o_hbm.at[b, pl.ds(step * TT, TT), :],
                                     sem_o.at[sl])

    def start_copies(copy, step, sl):
        for b in range(BATCH):
            copy(step, sl, b).start()

    def wait_slot(buf, sem, sl):
        pltpu.make_async_copy(buf.at[sl], buf.at[sl], sem.at[sl]).wait()

    def x_rows(sl, r0, nrows, c0=0, ncols=D_MODEL):
        t0, nt = r0 // BATCH, nrows // BATCH
        return xbuf[sl, pl.ds(t0, nt), :, c0:c0 + ncols].reshape(nrows, ncols)

    def rmsnorm(xv, g_ref, cast):
        ms = jnp.mean(xv * xv, axis=-1, keepdims=True)
        return ((xv * lax.rsqrt(ms + EPS)) * g_ref[...]).astype(cast)

    def win(off, c):
        return win_ref[:, off + c * CW:off + (c + 1) * CW]

    def norm_in(xsl, sl):
        for r0 in range(0, ROWS, NORM_RT):
            hb_in[sl, pl.ds(r0, NORM_RT), :] = rmsnorm(x_rows(xsl, r0, NORM_RT), ng_ref, BF16)

    def dots_xa_ga(sl, c):
        xa_ext[c, CONV_TAIL:CONV_TAIL + ROWS, :] = _dot(hb_in[sl], win(OFF_XA, c))
        tb[TB_GA + c] = _dot(hb_in[sl], win(OFF_GA, c))

    def weight_tiles():
        tiles = []

        def add(src, dst, nrows, ncols, scale):
            for r0 in range(0, nrows, ROWS):
                nr = min(ROWS, nrows - r0)
                for c0 in range(0, ncols, CW):
                    tiles.append((src.at[0, pl.ds(r0, nr), pl.ds(c0, CW)], nr, CW,
                                  dst.at[pl.ds(r0, nr), pl.ds(c0, CW)],
                                  scale(c0) if callable(scale) else scale))

        halved_in = lambda c0: 0.5 if (OFF_GA <= c0 < OFF_XB or c0 >= OFF_GB) else 1.0
        add(win_hbm, win_ref, D_MODEL, IN_COLS, halved_in)
        add(plru_hbm, plru_ref, LRU_WIDTH, D_MODEL, 0.5)
        add(ppool_hbm, ppool_ref, POOL_WIDTH, D_MODEL, 0.5)
        add(wout_hbm, wout_ref, D_MODEL, D_MODEL, 1.0)
        add(wpg_hbm, wpg_ref, D_MODEL, D_MODEL, 0.5)
        add(wpe_hbm, wpe_ref, P_DIM, D_MODEL, 0.5)
        for h in range(LRU_HEADS):
            for part, src in enumerate((wa_hbm, wx_hbm)):
                tiles.append((src.at[0, h], LRU_HEAD_DIM, LRU_HEAD_DIM,
                              wax_ref.at[h, :, pl.ds(part * LRU_HEAD_DIM, LRU_HEAD_DIM)], 0.5))
        for g in range(POOL_GROUPS):
            tiles.append((pw_hbm.at[0, g], POOL_GROUP_DIM, POOL_GROUP_DIM, pw_ref.at[g], 1.0))
        return tiles

    def load_weights():
        tiles = weight_tiles()
        copies = [pltpu.make_async_copy(src, tb.at[i % N_TB, pl.ds(0, nr), pl.ds(0, nc)], sem_w.at[i % N_TB])
                  for i, (src, nr, nc, _, _) in enumerate(tiles)]
        for i in range(len(tiles) + N_TB):
            done = i - N_TB
            if done >= 0:
                _, nr, nc, dst, scale = tiles[done]
                copies[done].wait()
                v = tb[done % N_TB, 0:nr, 0:nc]
                dst[...] = (v if scale == 1.0 else v * scale).astype(BF16)
            if i < len(tiles):
                copies[i].start()

    @pl.when(ti == 0)
    def _():
        start_copies(x_copy, 0, 0)
        start_copies(x_copy, 1, 1)
        start_copies(p_copy, 0, 0)
        start_copies(p_copy, 1, 1)
        load_weights()
        xa_ext[:, 0:CONV_TAIL, :] = jnp.zeros((N_CHUNKS, CONV_TAIL, CW), F32)
        xb_ext[0:POOL_TAIL, :] = jnp.zeros((POOL_TAIL, POOL_WIDTH), F32)
        hst[...] = jnp.zeros((BATCH, LRU_WIDTH), F32)
        wait_slot(xbuf, sem_x, 0)
        norm_in(0, 0)
        dots_xa_ga(0, 0)
        dots_xa_ga(0, 1)

    @pl.when(ti >= 1)
    def _():
        start_copies(o_copy, ti - 1, nslot)

    @pl.when(ti >= 2)
    def _():
        wait_slot(obuf, sem_o, slot)

    wait_slot(pbuf, sem_p, xs)
    wait_slot(xbuf, sem_x, xs1)

    hb = hb_in.at[slot]
    pbf[...] = pbuf[xs].reshape(ROWS, P_DIM).astype(BF16)

    lam = lam_ref[...]
    half_c = (-0.5 * LRU_C) * (jnp.maximum(-lam, 0.0) + jnp.log1p(jnp.exp(-jnp.abs(lam))))

    def conv(c):
        q, c0 = c % 2, c * CW
        acc = cb_ref[:, c0:c0 + CW]
        for k in range(CONV_WIDTH):
            acc = acc + xa_ext[c, k * BATCH:k * BATCH + ROWS, :] * cw_ref[k:k + 1, c0:c0 + CW]
        xcb[q] = acc.astype(BF16)
        xa_ext[c, 0:CONV_TAIL, :] = xa_ext[c, ROWS:ROWS + CONV_TAIL, :]
        return acc

    def dots_gates(c):
        q = c % 2
        for hh, tg in enumerate((TB_G0, TB_G1)):
            l0 = hh * LRU_HEAD_DIM
            tb[tg + q] = _dot(xcb[q, :, l0:l0 + LRU_HEAD_DIM], wax_ref[c * HEADS_PER_CHUNK + hh])

    def gates(c, xc):
        q = c % 2
        for hh, tg in enumerate((TB_G0, TB_G1)):
            l0 = hh * LRU_HEAD_DIM
            ch = slice(c * CW + l0, c * CW + l0 + LRU_HEAD_DIM)
            lanes = slice(l0, l0 + LRU_HEAD_DIM)
            r_tanh = jnp.tanh(tb[tg + q, :, 0:LRU_HEAD_DIM] + 0.5 * ba_ref[:, ch])
            i_gate = 0.5 * jnp.tanh(tb[tg + q, :, LRU_HEAD_DIM:2 * LRU_HEAD_DIM] + 0.5 * bx_ref[:, ch]) + 0.5
            a = jnp.exp(half_c[:, ch] * (r_tanh + 1.0))
            y = 1.0 - a * a
            mult = y * lax.rsqrt(jnp.maximum(y, TINY))
            tb[TB_A + q, :, lanes] = a
            tb[TB_U + q, :, lanes] = mult * (i_gate * xc[:, lanes])

    def scan(c):
        q, c0 = c % 2, c * CW
        h = hst[:, c0:c0 + CW]
        for t in range(TT):
            rows = pl.ds(t * BATCH, BATCH)
            h = tb[TB_A + q, rows, :] * h + tb[TB_U + q, rows, :]
            ga = tb[TB_GA + c, rows, :]
            ya[rows, c0:c0 + CW] = (h * (_tanh_p1(ga) * ga)).astype(BF16)
        hst[:, c0:c0 + CW] = h

    def dots_pool_in():
        xb_ext[POOL_TAIL:POOL_TAIL + ROWS, :] = _dot(hb[...], win_ref[:, OFF_XB:OFF_XB + POOL_WIDTH])
        for half in range(POOL_WIDTH // CW):
            tb[TB_GB + half] = _dot(hb[...], win(OFF_GB, half))

    def pool_windows(g):
        k = POOL_WINDOWS[g]
        lanes = slice(g * POOL_GROUP_DIM, (g + 1) * POOL_GROUP_DIM)
        cur = xb_ext[POOL_TAIL:POOL_TAIL + ROWS, lanes]
        s = cur
        for j in range(1, k):
            s = s + xb_ext[POOL_TAIL - j * BATCH:POOL_TAIL - j * BATCH + ROWS, lanes]
        row = lax.broadcasted_iota(jnp.int32, (ROWS, POOL_GROUP_DIM), 0)
        pos = ti * TT + row // BATCH
        cnt = jnp.minimum(pos + 1, k).astype(F32)
        dbuf[g] = (s / cnt - cur).astype(BF16)

    def pool_out(g):
        lanes = slice(g * POOL_GROUP_DIM, (g + 1) * POOL_GROUP_DIM)
        half, gg = divmod(g, CW // POOL_GROUP_DIM)
        gb = tb[TB_GB + half, :, gg * POOL_GROUP_DIM:(gg + 1) * POOL_GROUP_DIM]
        y = _dot(dbuf[g], pw_ref[g]) * ps_ref[:, lanes]
        yb[:, lanes] = (y * (_tanh_p1(gb) * gb)).astype(BF16)

    def dots_merge_logits(c):
        tb[TB_MA + c] = _dot(hb[...], win(OFF_MA, c))
        tb[TB_MB + c] = _dot(hb[...], win(OFF_MB, c))

    for c in range(N_CHUNKS):
        xc = conv(c)
        dots_gates(c)
        if c + 2 < N_CHUNKS:
            dots_xa_ga(slot, c + 2)
        if c == 0:
            dots_pool_in()
        else:
            dots_merge_logits(c - 1)
        gates(c, xc)
        scan(c)
        pool_windows(c)
        pool_out(c)
    dots_merge_logits(N_CHUNKS - 1)
    xb_ext[0:POOL_TAIL, :] = xb_ext[ROWS:ROWS + POOL_TAIL, :]

    for c in range(N_CHUNKS):
        c0 = c * CW
        pa = _dot(ya[...], plru_ref[:, c0:c0 + CW])
        pb = _dot(yb[...], ppool_ref[:, c0:c0 + CW])
        mbuf[:, c0:c0 + CW] = (_tanh_p1(tb[TB_MA + c]) * pa + _tanh_p1(tb[TB_MB + c]) * pb).astype(BF16)

    for c in range(N_CHUNKS):
        c0 = c * CW
        x1[:, c0:c0 + CW] = x_rows(xs, 0, ROWS, c0, CW) + _dot(mbuf[...], wout_ref[:, c0:c0 + CW])
    norm_in(xs1, nslot)
    start_copies(x_copy, nxt2, xs2)
    start_copies(p_copy, nxt2, xs2)

    for c in range(N_CHUNKS):
        tb[TB_PE + c] = _dot(pbf[...], wpe_ref[:, c * CW:(c + 1) * CW])
    for r0 in range(0, ROWS, NORM_RT):
        hb2[pl.ds(r0, NORM_RT), :] = rmsnorm(x1[pl.ds(r0, NORM_RT), :], pg_ref, BF16)

    for c in range(N_CHUNKS):
        c0 = c * CW
        gate = _dot(hb2[...], wpg_ref[:, c0:c0 + CW])
        x1[:, c0:c0 + CW] = x1[:, c0:c0 + CW] + _tanh_p1(gate) * tb[TB_PE + c]
    dots_xa_ga(nslot, 0)
    dots_xa_ga(nslot, 1)

    for r0 in range(0, ROWS, NORM_RT):
        ov = rmsnorm(x1[pl.ds(r0, NORM_RT), :], fg_ref, F32)
        obuf[slot, pl.ds(r0 // BATCH, NORM_RT // BATCH), :, :] = ov.reshape(NORM_RT // BATCH, BATCH, D_MODEL)

    @pl.when(ti == n_steps - 1)
    def _():
        start_copies(o_copy, ti, slot)
        wait_slot(xbuf, sem_x, xs2)
        wait_slot(pbuf, sem_p, xs1)
        wait_slot(pbuf, sem_p, xs2)
        wait_slot(obuf, sem_o, nslot)
        wait_slot(obuf, sem_o, slot)


def _const_spec(shape):
    zeros = (0,) * len(shape)
    return pl.BlockSpec(shape, lambda i: zeros, pipeline_mode=pl.Buffered(1))


@jax.jit
def kernel(x, p, norm_g, w_in, conv_w, conv_b, lru_w_a, lru_b_a, lru_w_x, lru_b_x, lru_lambda,
           pool_w, pool_scale, w_proj_lru, w_proj_pool, w_out, ple_norm_g, w_ple_gate,
           w_ple_proj, final_g):
    assert x.shape == (BATCH, SEQ, D_MODEL) and p.shape == (1, BATCH, SEQ, P_DIM)
    row = lambda v: v.reshape(1, -1)
    operands = [
        x, p, row(norm_g[0]), w_in, conv_w[0], row(conv_b[0]), lru_w_a, row(lru_b_a[0]), lru_w_x,
        row(lru_b_x[0]), row(lru_lambda[0]), pool_w, row(pool_scale[0]), w_proj_lru, w_proj_pool,
        w_out, row(ple_norm_g[0]), w_ple_gate, w_ple_proj, row(final_g),
    ]
    in_specs = [pl.BlockSpec(memory_space=pl.ANY) if v.ndim > 2 else _const_spec(v.shape)
                for v in operands]
    scratch = [
        pltpu.VMEM((X_SLOTS, TT, BATCH, D_MODEL), F32),
        pltpu.VMEM((X_SLOTS, TT, BATCH, P_DIM), F32),
        pltpu.VMEM((2, TT, BATCH, D_MODEL), F32),
        pltpu.SemaphoreType.DMA((X_SLOTS,)),
        pltpu.SemaphoreType.DMA((X_SLOTS,)),
        pltpu.SemaphoreType.DMA((2,)),
        pltpu.SemaphoreType.DMA((N_TB,)),
        pltpu.VMEM((D_MODEL, IN_COLS), BF16),
        pltpu.VMEM((LRU_HEADS, LRU_HEAD_DIM, 2 * LRU_HEAD_DIM), BF16),
        pltpu.VMEM((POOL_GROUPS, POOL_GROUP_DIM, POOL_GROUP_DIM), BF16),
        pltpu.VMEM((LRU_WIDTH, D_MODEL), BF16),
        pltpu.VMEM((POOL_WIDTH, D_MODEL), BF16),
        pltpu.VMEM((D_MODEL, D_MODEL), BF16),
        pltpu.VMEM((D_MODEL, D_MODEL), BF16),
        pltpu.VMEM((P_DIM, D_MODEL), BF16),
        pltpu.VMEM((2, ROWS, D_MODEL), BF16),
        pltpu.VMEM((ROWS, D_MODEL), BF16),
        pltpu.VMEM((ROWS, P_DIM), BF16),
        pltpu.VMEM((N_CHUNKS, ROWS + CONV_TAIL, CW), F32),
        pltpu.VMEM((ROWS + POOL_TAIL, POOL_WIDTH), F32),
        pltpu.VMEM((BATCH, LRU_WIDTH), F32),
        pltpu.VMEM((N_TB, ROWS, CW), F32),
        pltpu.VMEM((2, ROWS, CW), BF16),
        pltpu.VMEM((ROWS, LRU_WIDTH), BF16),
        pltpu.VMEM((ROWS, POOL_WIDTH), BF16),
        pltpu.VMEM((POOL_GROUPS, ROWS, POOL_GROUP_DIM), BF16),
        pltpu.VMEM((ROWS, D_MODEL), BF16),
        pltpu.VMEM((ROWS, D_MODEL), F32),
    ]
    scratch_vmem_bytes = sum(math.prod(s.shape) * jnp.dtype(s.dtype).itemsize
                             for s in scratch if s.memory_space == pltpu.VMEM)
    return pl.pallas_call(
        _block_kernel,
        grid=(SEQ // TT,),
        in_specs=in_specs,
        out_specs=pl.BlockSpec(memory_space=pl.ANY),
        out_shape=jax.ShapeDtypeStruct((BATCH, SEQ, D_MODEL), F32),
        scratch_shapes=scratch,
        compiler_params=pltpu.CompilerParams(
            dimension_semantics=("arbitrary",),
            vmem_limit_bytes=scratch_vmem_bytes + COMPILER_VMEM_ALLOWANCE_BYTES,
        ),
        name="rglru_pool_block",
    )(*operands)
```

```python
import math

import jax
import jax.numpy as jnp
from jax import lax
from jax.experimental import pallas as pl
from jax.experimental.pallas import tpu as pltpu

D_MODEL = 1024
BATCH = 16
SEQ = 2048
P_DIM = 256
LRU_WIDTH = 1024
LRU_HEADS = 8
LRU_HEAD_DIM = 128
CONV_WIDTH = 4
LRU_C = 8.0
POOL_WIDTH = 512
POOL_WINDOWS = (2, 4, 8, 16)
POOL_GROUPS = 4
POOL_GROUP_DIM = 128
MAX_WIN = 16
EPS = 1e-6

OFF_XA = 0
OFF_GA = LRU_WIDTH
OFF_XB = 2 * LRU_WIDTH
OFF_GB = OFF_XB + POOL_WIDTH
OFF_MA = OFF_GB + POOL_WIDTH
OFF_MB = OFF_MA + D_MODEL
IN_COLS = OFF_MB + D_MODEL

TT = 32
ROWS = TT * BATCH
CW = 256
N_CHUNKS = D_MODEL // CW
HEADS_PER_CHUNK = CW // LRU_HEAD_DIM
NORM_RT = 32
CONV_TAIL = (CONV_WIDTH - 1) * BATCH
POOL_TAIL = MAX_WIN * BATCH
X_SLOTS = 3
COMPILER_VMEM_ALLOWANCE_BYTES = 4 * 1024 * 1024

TB_GA = 0
TB_G0, TB_G1, TB_A, TB_U = 6, 8, 10, 12
TB_MA, TB_MB = 14, 18
TB_GB = 22
N_TB = 24

F32 = jnp.float32
BF16 = jnp.bfloat16
TINY = 1e-30


def _tanh_p1(vh):
    return jnp.tanh(vh) + 1.0


def _dot(a, b):
    return jnp.dot(a, b, preferred_element_type=F32)


def _block_kernel(x_hbm, p_hbm, ng_ref, win_hbm, cw_ref, cb_ref, wa_hbm, ba_ref, wx_hbm, bx_ref,
                  lam_ref, pw_hbm, ps_ref, plru_hbm, ppool_hbm, wout_hbm, pg_ref, wpg_hbm,
                  wpe_hbm, fg_ref,
                  o_hbm,
                  xbuf, pbuf, obuf, sem_x, sem_p, sem_o, sem_w,
                  win_ref, wax_ref, pw_ref, plru_ref, ppool_ref, wout_ref, wpg_ref, wpe_ref,
                  hb_in, hb2, pbf, xa_ext, xb_ext, hst, tb, ya, yb, dbuf, mbuf, x1):
    ti = pl.program_id(0)
    n_steps = pl.num_programs(0)
    slot = ti % 2
    nxt = jnp.minimum(ti + 1, n_steps - 1)
    nxt2 = jnp.minimum(ti + 2, n_steps - 1)
    nslot = 1 - slot
    xs, xs1, xs2 = ti % X_SLOTS, (ti + 1) % X_SLOTS, (ti + 2) % X_SLOTS

    def x_copy(step, sl, b):
        return pltpu.make_async_copy(x_hbm.at[b, pl.ds(step * TT, TT), :], xbuf.at[sl, :, b, :],
                                     sem_x.at[sl])

    def p_copy(step, sl, b):
        return pltpu.make_async_copy(p_hbm.at[0, b, pl.ds(step * TT, TT), :], pbuf.at[sl, :, b, :],
                                     sem_p.at[sl])

    def o_copy(step, sl, b):
        return pltpu.make_async_copy(obuf.at[sl, :, b, :], o_hbm.at[b, pl.ds(step * TT, TT), :],
                                     sem_o.at[sl])

    def start_copies(copy, step, sl):
        for b in range(BATCH):
            copy(step, sl, b).start()

    def wait_slot(buf, sem, sl):
        pltpu.make_async_copy(buf.at[sl], buf.at[sl], sem.at[sl]).wait()

    def x_rows(sl, r0, nrows, c0=0, ncols=D_MODEL):
        t0, nt = r0 // BATCH, nrows // BATCH
        return xbuf[sl, pl.ds(t0, nt), :, c0:c0 + ncols].reshape(nrows, ncols)

    def rmsnorm(xv, g_ref, cast):
        ms = jnp.mean(xv * xv, axis=-1, keepdims=True)
        return ((xv * lax.rsqrt(ms + EPS)) * g_ref[...]).astype(cast)

    def win(off, c):
        return win_ref[:, off + c * CW:off + (c + 1) * CW]

    def norm_in(xsl, sl):
        for r0 in range(0, ROWS, NORM_RT):
            hb_in[sl, pl.ds(r0, NORM_RT), :] = rmsnorm(x_rows(xsl, r0, NORM_RT), ng_ref, BF16)

    def dots_xa_ga(sl, c):
        xa_ext[c, CONV_TAIL:CONV_TAIL + ROWS, :] = _dot(hb_in[sl], win(OFF_XA, c))
        tb[TB_GA + c] = _dot(hb_in[sl], win(OFF_GA, c))

    def weight_tiles():
        tiles = []

        def add(src, dst, nrows, ncols, scale):
            for r0 in range(0, nrows, ROWS):
                nr = min(ROWS, nrows - r0)
                for c0 in range(0, ncols, CW):
                    tiles.append((src.at[0, pl.ds(r0, nr), pl.ds(c0, CW)], nr, CW,
                                  dst.at[pl.ds(r0, nr), pl.ds(c0, CW)],
                                  scale(c0) if callable(scale) else scale))

        halved_in = lambda c0: 0.5 if (OFF_GA <= c0 < OFF_XB or c0 >= OFF_GB) else 1.0
        add(win_hbm, win_ref, D_MODEL, IN_COLS, halved_in)
        add(plru_hbm, plru_ref, LRU_WIDTH, D_MODEL, 0.5)
        add(ppool_hbm, ppool_ref, POOL_WIDTH, D_MODEL, 0.5)
        add(wout_hbm, wout_ref, D_MODEL, D_MODEL, 1.0)
        add(wpg_hbm, wpg_ref, D_MODEL, D_MODEL, 0.5)
        add(wpe_hbm, wpe_ref, P_DIM, D_MODEL, 0.5)
        for h in range(LRU_HEADS):
            for part, src in enumerate((wa_hbm, wx_hbm)):
                tiles.append((src.at[0, h], LRU_HEAD_DIM, LRU_HEAD_DIM,
                              wax_ref.at[h, :, pl.ds(part * LRU_HEAD_DIM, LRU_HEAD_DIM)], 0.5))
        for g in range(POOL_GROUPS):
            tiles.append((pw_hbm.at[0, g], POOL_GROUP_DIM, POOL_GROUP_DIM, pw_ref.at[g], 1.0))
        return tiles

    def load_weights():
        tiles = weight_tiles()
        copies = [pltpu.make_async_copy(src, tb.at[i % N_TB, pl.ds(0, nr), pl.ds(0, nc)], sem_w.at[i % N_TB])
                  for i, (src, nr, nc, _, _) in enumerate(tiles)]
        for i in range(len(tiles) + N_TB):
            done = i - N_TB
            if done >= 0:
                _, nr, nc, dst, scale = tiles[done]
                copies[done].wait()
                v = tb[done % N_TB, 0:nr, 0:nc]
                dst[...] = (v if scale == 1.0 else v * scale).astype(BF16)
            if i < len(tiles):
                copies[i].start()

    @pl.when(ti == 0)
    def _():
        start_copies(x_copy, 0, 0)
        start_copies(x_copy, 1, 1)
        start_copies(p_copy, 0, 0)
        start_copies(p_copy, 1, 1)
        load_weights()
        xa_ext[:, 0:CONV_TAIL, :] = jnp.zeros((N_CHUNKS, CONV_TAIL, CW), F32)
        xb_ext[0:POOL_TAIL, :] = jnp.zeros((POOL_TAIL, POOL_WIDTH), F32)
        hst[...] = jnp.zeros((BATCH, LRU_WIDTH), F32)
        wait_slot(xbuf, sem_x, 0)
        norm_in(0, 0)
        dots_xa_ga(0, 0)
        dots_xa_ga(0, 1)

    @pl.when(ti >= 1)
    def _():
        start_copies(o_copy, ti - 1, nslot)

    @pl.when(ti >= 2)
    def _():
        wait_slot(obuf, sem_o, slot)

    wait_slot(pbuf, sem_p, xs)
    wait_slot(xbuf, sem_x, xs1)

    hb = hb_in.at[slot]
    pbf[...] = pbuf[xs].reshape(ROWS, P_DIM).astype(BF16)

    lam = lam_ref[...]
    half_c = (-0.5 * LRU_C) * (jnp.maximum(-lam, 0.0) + jnp.log1p(jnp.exp(-jnp.abs(lam))))

    def conv(c):
        c0 = c * CW
        acc = cb_ref[:, c0:c0 + CW]
        for k in range(CONV_WIDTH):
            acc = acc + xa_ext[c, k * BATCH:k * BATCH + ROWS, :] * cw_ref[k:k + 1, c0:c0 + CW]
        xa_ext[c, 0:CONV_TAIL, :] = xa_ext[c, ROWS:ROWS + CONV_TAIL, :]
        return acc

    def dots_gates(c, xc):
        q = c % 2
        xcb = xc.astype(BF16)
        for hh, tg in enumerate((TB_G0, TB_G1)):
            l0 = hh * LRU_HEAD_DIM
            tb[tg + q] = _dot(xcb[:, l0:l0 + LRU_HEAD_DIM], wax_ref[c * HEADS_PER_CHUNK + hh])

    def gates(c, xc):
        q = c % 2
        for hh, tg in enumerate((TB_G0, TB_G1)):
            l0 = hh * LRU_HEAD_DIM
            ch = slice(c * CW + l0, c * CW + l0 + LRU_HEAD_DIM)
            lanes = slice(l0, l0 + LRU_HEAD_DIM)
            r_tanh = jnp.tanh(tb[tg + q, :, 0:LRU_HEAD_DIM] + 0.5 * ba_ref[:, ch])
            i_gate = 0.5 * jnp.tanh(tb[tg + q, :, LRU_HEAD_DIM:2 * LRU_HEAD_DIM] + 0.5 * bx_ref[:, ch]) + 0.5
            a = jnp.exp(half_c[:, ch] * (r_tanh + 1.0))
            y = 1.0 - a * a
            mult = y * lax.rsqrt(jnp.maximum(y, TINY))
            tb[TB_A + q, :, lanes] = a
            tb[TB_U + q, :, lanes] = mult * (i_gate * xc[:, lanes])

    def scan(c):
        q, c0 = c % 2, c * CW
        h = hst[:, c0:c0 + CW]
        for t in range(TT):
            rows = pl.ds(t * BATCH, BATCH)
            h = tb[TB_A + q, rows, :] * h + tb[TB_U + q, rows, :]
            ga = tb[TB_GA + c, rows, :]
            ya[rows, c0:c0 + CW] = (h * (_tanh_p1(ga) * ga)).astype(BF16)
        hst[:, c0:c0 + CW] = h

    def dots_pool_in():
        xb_ext[POOL_TAIL:POOL_TAIL + ROWS, :] = _dot(hb[...], win_ref[:, OFF_XB:OFF_XB + POOL_WIDTH])
        for half in range(POOL_WIDTH // CW):
            tb[TB_GB + half] = _dot(hb[...], win(OFF_GB, half))

    def pool_windows(g):
        k = POOL_WINDOWS[g]
        lanes = slice(g * POOL_GROUP_DIM, (g + 1) * POOL_GROUP_DIM)
        cur = xb_ext[POOL_TAIL:POOL_TAIL + ROWS, lanes]
        s = cur
        for j in range(1, k):
            s = s + xb_ext[POOL_TAIL - j * BATCH:POOL_TAIL - j * BATCH + ROWS, lanes]
        row = lax.broadcasted_iota(jnp.int32, (ROWS, POOL_GROUP_DIM), 0)
        pos = ti * TT + row // BATCH
        cnt = jnp.minimum(pos + 1, k).astype(F32)
        dbuf[g] = (s / cnt - cur).astype(BF16)

    def pool_out(g):
        lanes = slice(g * POOL_GROUP_DIM, (g + 1) * POOL_GROUP_DIM)
        half, gg = divmod(g, CW // POOL_GROUP_DIM)
        gb = tb[TB_GB + half, :, gg * POOL_GROUP_DIM:(gg + 1) * POOL_GROUP_DIM]
        y = _dot(dbuf[g], pw_ref[g]) * ps_ref[:, lanes]
        yb[:, lanes] = (y * (_tanh_p1(gb) * gb)).astype(BF16)

    def dots_merge_logits(c):
        tb[TB_MA + c] = _dot(hb[...], win(OFF_MA, c))
        tb[TB_MB + c] = _dot(hb[...], win(OFF_MB, c))

    for c in range(N_CHUNKS):
        xc = conv(c)
        dots_gates(c, xc)
        if c + 2 < N_CHUNKS:
            dots_xa_ga(slot, c + 2)
        if c == 0:
            dots_pool_in()
        else:
            dots_merge_logits(c - 1)
        gates(c, xc)
        scan(c)
        pool_windows(c)
        pool_out(c)
    dots_merge_logits(N_CHUNKS - 1)
    xb_ext[0:POOL_TAIL, :] = xb_ext[ROWS:ROWS + POOL_TAIL, :]

    for c in range(N_CHUNKS):
        c0 = c * CW
        pa = _dot(ya[...], plru_ref[:, c0:c0 + CW])
        pb = _dot(yb[...], ppool_ref[:, c0:c0 + CW])
        mbuf[:, c0:c0 + CW] = (_tanh_p1(tb[TB_MA + c]) * pa + _tanh_p1(tb[TB_MB + c]) * pb).astype(BF16)

    for c in range(N_CHUNKS):
        c0 = c * CW
        x1[:, c0:c0 + CW] = x_rows(xs, 0, ROWS, c0, CW) + _dot(mbuf[...], wout_ref[:, c0:c0 + CW])
    norm_in(xs1, nslot)
    start_copies(x_copy, nxt2, xs2)
    start_copies(p_copy, nxt2, xs2)

    pe = [_dot(pbf[...], wpe_ref[:, c * CW:(c + 1) * CW]) for c in range(N_CHUNKS)]
    for r0 in range(0, ROWS, NORM_RT):
        hb2[pl.ds(r0, NORM_RT), :] = rmsnorm(x1[pl.ds(r0, NORM_RT), :], pg_ref, BF16)

    for c in range(N_CHUNKS):
        c0 = c * CW
        gate = _dot(hb2[...], wpg_ref[:, c0:c0 + CW])
        x1[:, c0:c0 + CW] = x1[:, c0:c0 + CW] + _tanh_p1(gate) * pe[c]
    dots_xa_ga(nslot, 0)
    dots_xa_ga(nslot, 1)

    for r0 in range(0, ROWS, NORM_RT):
        ov = rmsnorm(x1[pl.ds(r0, NORM_RT), :], fg_ref, F32)
        obuf[slot, pl.ds(r0 // BATCH, NORM_RT // BATCH), :, :] = ov.reshape(NORM_RT // BATCH, BATCH, D_MODEL)

    @pl.when(ti == n_steps - 1)
    def _():
        start_copies(o_copy, ti, slot)
        wait_slot(xbuf, sem_x, xs2)
        wait_slot(pbuf, sem_p, xs1)
        wait_slot(pbuf, sem_p, xs2)
        wait_slot(obuf, sem_o, nslot)
        wait_slot(obuf, sem_o, slot)


def _const_spec(shape):
    zeros = (0,) * len(shape)
    return pl.BlockSpec(shape, lambda i: zeros, pipeline_mode=pl.Buffered(1))


@jax.jit
def kernel(x, p, norm_g, w_in, conv_w, conv_b, lru_w_a, lru_b_a, lru_w_x, lru_b_x, lru_lambda,
           pool_w, pool_scale, w_proj_lru, w_proj_pool, w_out, ple_norm_g, w_ple_gate,
           w_ple_proj, final_g):
    assert x.shape == (BATCH, SEQ, D_MODEL) and p.shape == (1, BATCH, SEQ, P_DIM)
    row = lambda v: v.reshape(1, -1)
    operands = [
        x, p, row(norm_g[0]), w_in, conv_w[0], row(conv_b[0]), lru_w_a, row(lru_b_a[0]), lru_w_x,
        row(lru_b_x[0]), row(lru_lambda[0]), pool_w, row(pool_scale[0]), w_proj_lru, w_proj_pool,
        w_out, row(ple_norm_g[0]), w_ple_gate, w_ple_proj, row(final_g),
    ]
    in_specs = [pl.BlockSpec(memory_space=pl.ANY) if v.ndim > 2 else _const_spec(v.shape)
                for v in operands]
    scratch = [
        pltpu.VMEM((X_SLOTS, TT, BATCH, D_MODEL), F32),
        pltpu.VMEM((X_SLOTS, TT, BATCH, P_DIM), F32),
        pltpu.VMEM((2, TT, BATCH, D_MODEL), F32),
        pltpu.SemaphoreType.DMA((X_SLOTS,)),
        pltpu.SemaphoreType.DMA((X_SLOTS,)),
        pltpu.SemaphoreType.DMA((2,)),
        pltpu.SemaphoreType.DMA((N_TB,)),
        pltpu.VMEM((D_MODEL, IN_COLS), BF16),
        pltpu.VMEM((LRU_HEADS, LRU_HEAD_DIM, 2 * LRU_HEAD_DIM), BF16),
        pltpu.VMEM((POOL_GROUPS, POOL_GROUP_DIM, POOL_GROUP_DIM), BF16),
        pltpu.VMEM((LRU_WIDTH, D_MODEL), BF16),
        pltpu.VMEM((POOL_WIDTH, D_MODEL), BF16),
        pltpu.VMEM((D_MODEL, D_MODEL), BF16),
        pltpu.VMEM((D_MODEL, D_MODEL), BF16),
        pltpu.VMEM((P_DIM, D_MODEL), BF16),
        pltpu.VMEM((2, ROWS, D_MODEL), BF16),
        pltpu.VMEM((ROWS, D_MODEL), BF16),
        pltpu.VMEM((ROWS, P_DIM), BF16),
        pltpu.VMEM((N_CHUNKS, ROWS + CONV_TAIL, CW), F32),
        pltpu.VMEM((ROWS + POOL_TAIL, POOL_WIDTH), F32),
        pltpu.VMEM((BATCH, LRU_WIDTH), F32),
        pltpu.VMEM((N_TB, ROWS, CW), F32),
        pltpu.VMEM((ROWS, LRU_WIDTH), BF16),
        pltpu.VMEM((ROWS, POOL_WIDTH), BF16),
        pltpu.VMEM((POOL_GROUPS, ROWS, POOL_GROUP_DIM), BF16),
        pltpu.VMEM((ROWS, D_MODEL), BF16),
        pltpu.VMEM((ROWS, D_MODEL), F32),
    ]
    scratch_vmem_bytes = sum(math.prod(s.shape) * jnp.dtype(s.dtype).itemsize
                             for s in scratch if s.memory_space == pltpu.VMEM)
    return pl.pallas_call(
        _block_kernel,
        grid=(SEQ // TT,),
        in_specs=in_specs,
        out_specs=pl.BlockSpec(memory_space=pl.ANY),
        out_shape=jax.ShapeDtypeStruct((BATCH, SEQ, D_MODEL), F32),
        scratch_shapes=scratch,
        compiler_params=pltpu.CompilerParams(
            dimension_semantics=("arbitrary",),
            vmem_limit_bytes=scratch_vmem_bytes + COMPILER_VMEM_ALLOWANCE_BYTES,
        ),
        name="rglru_pool_block",
    )(*operands)
```

```python
import math

import jax
import jax.numpy as jnp
from jax import lax
from jax.experimental import pallas as pl
from jax.experimental.pallas import tpu as pltpu

D_MODEL = 1024
BATCH = 16
SEQ = 2048
P_DIM = 256
LRU_WIDTH = 1024
LRU_HEADS = 8
LRU_HEAD_DIM = 128
CONV_WIDTH = 4
LRU_C = 8.0
POOL_WIDTH = 512
POOL_WINDOWS = (2, 4, 8, 16)
POOL_GROUPS = 4
POOL_GROUP_DIM = 128
MAX_WIN = 16
EPS = 1e-6

OFF_XA = 0
OFF_GA = LRU_WIDTH
OFF_XB = 2 * LRU_WIDTH
OFF_GB = OFF_XB + POOL_WIDTH
OFF_MA = OFF_GB + POOL_WIDTH
OFF_MB = OFF_MA + D_MODEL
IN_COLS = OFF_MB + D_MODEL

TT = 32
ROWS = TT * BATCH
CW = 256
N_CHUNKS = D_MODEL // CW
HEADS_PER_CHUNK = CW // LRU_HEAD_DIM
NORM_RT = 32
CONV_TAIL = (CONV_WIDTH - 1) * BATCH
POOL_TAIL = MAX_WIN * BATCH
X_SLOTS = 3
COMPILER_VMEM_ALLOWANCE_BYTES = 4 * 1024 * 1024

TB_GA = 0
TB_G0, TB_G1, TB_A, TB_U = 6, 8, 10, 12
TB_MA, TB_MB = 14, 18
TB_GB = 22
N_TB = 24
TB_PE = 6

F32 = jnp.float32
BF16 = jnp.bfloat16
TINY = 1e-30


def _tanh_p1(vh):
    return jnp.tanh(vh) + 1.0


def _dot(a, b):
    return jnp.dot(a, b, preferred_element_type=F32)


def _block_kernel(x_hbm, p_hbm, ng_ref, win_hbm, cw_ref, cb_ref, wa_hbm, ba_ref, wx_hbm, bx_ref,
                  lam_ref, pw_hbm, ps_ref, plru_hbm, ppool_hbm, wout_hbm, pg_ref, wpg_hbm,
                  wpe_hbm, fg_ref,
                  o_hbm,
                  xbuf, pbuf, obuf, sem_x, sem_p, sem_o, sem_w,
                  win_ref, wax_ref, pw_ref, plru_ref, ppool_ref, wout_ref, wpg_ref, wpe_ref,
                  hb_in, hb2, pbf, xa_ext, xb_ext, hst, tb, xcb, ya, yb, dbuf, mbuf, x1):
    ti = pl.program_id(0)
    n_steps = pl.num_programs(0)
    slot = ti % 2
    nxt = jnp.minimum(ti + 1, n_steps - 1)
    nxt2 = jnp.minimum(ti + 2, n_steps - 1)
    nslot = 1 - slot
    xs, xs1, xs2 = ti % X_SLOTS, (ti + 1) % X_SLOTS, (ti + 2) % X_SLOTS

    def x_copy(step, sl, b):
        return pltpu.make_async_copy(x_hbm.at[b, pl.ds(step * TT, TT), :], xbuf.at[sl, :, b, :],
                                     sem_x.at[sl])

    def p_copy(step, sl, b):
        return pltpu.make_async_copy(p_hbm.at[0, b, pl.ds(step * TT, TT), :], pbuf.at[sl, :, b, :],
                                     sem_p.at[sl])

    def o_copy(step, sl, b):
        return pltpu.make_async_copy(obuf.at[sl, :, b, :], o_hbm.at[b, pl.ds(step * TT, TT), :],
                                     sem_o.at[sl])

    def start_copies(copy, step, sl):
        for b in range(BATCH):
            copy(step, sl, b).start()

    def wait_slot(buf, sem, sl):
        pltpu.make_async_copy(buf.at[sl], buf.at[sl], sem.at[sl]).wait()

    def x_rows(sl, r0, nrows, c0=0, ncols=D_MODEL):
        t0, nt = r0 // BATCH, nrows // BATCH
        return xbuf[sl, pl.ds(t0, nt), :, c0:c0 + ncols].reshape(nrows, ncols)

    def rmsnorm(xv, g_ref, cast):
        ms = jnp.mean(xv * xv, axis=-1, keepdims=True)
        return ((xv * lax.rsqrt(ms + EPS)) * g_ref[...]).astype(cast)

    def win(off, c):
        return win_ref[:, off + c * CW:off + (c + 1) * CW]

    def norm_in(xsl, sl):
        for r0 in range(0, ROWS, NORM_RT):
            hb_in[sl, pl.ds(r0, NORM_RT), :] = rmsnorm(x_rows(xsl, r0, NORM_RT), ng_ref, BF16)

    def dots_xa_ga(sl, c):
        xa_ext[c, CONV_TAIL:CONV_TAIL + ROWS, :] = _dot(hb_in[sl], win(OFF_XA, c))
        tb[TB_GA + c] = _dot(hb_in[sl], win(OFF_GA, c))

    def weight_tiles():
        tiles = []

        def add(src, dst, nrows, ncols, scale):
            for r0 in range(0, nrows, ROWS):
                nr = min(ROWS, nrows - r0)
                for c0 in range(0, ncols, CW):
                    tiles.append((src.at[0, pl.ds(r0, nr), pl.ds(c0, CW)], nr, CW,
                                  dst.at[pl.ds(r0, nr), pl.ds(c0, CW)],
                                  scale(c0) if callable(scale) else scale))

        halved_in = lambda c0: 0.5 if (OFF_GA <= c0 < OFF_XB or c0 >= OFF_GB) else 1.0
        add(win_hbm, win_ref, D_MODEL, IN_COLS, halved_in)
        add(plru_hbm, plru_ref, LRU_WIDTH, D_MODEL, 0.5)
        add(ppool_hbm, ppool_ref, POOL_WIDTH, D_MODEL, 0.5)
        add(wout_hbm, wout_ref, D_MODEL, D_MODEL, 1.0)
        add(wpg_hbm, wpg_ref, D_MODEL, D_MODEL, 0.5)
        add(wpe_hbm, wpe_ref, P_DIM, D_MODEL, 0.5)
        for h in range(LRU_HEADS):
            for part, src in enumerate((wa_hbm, wx_hbm)):
                tiles.append((src.at[0, h], LRU_HEAD_DIM, LRU_HEAD_DIM,
                              wax_ref.at[h, :, pl.ds(part * LRU_HEAD_DIM, LRU_HEAD_DIM)], 1.0))
        for g in range(POOL_GROUPS):
            tiles.append((pw_hbm.at[0, g], POOL_GROUP_DIM, POOL_GROUP_DIM, pw_ref.at[g], 1.0))
        return tiles

    def load_weights():
        tiles = weight_tiles()
        copies = [pltpu.make_async_copy(src, tb.at[i % N_TB, pl.ds(0, nr), pl.ds(0, nc)], sem_w.at[i % N_TB])
                  for i, (src, nr, nc, _, _) in enumerate(tiles)]
        for i in range(len(tiles) + N_TB):
            done = i - N_TB
            if done >= 0:
                _, nr, nc, dst, scale = tiles[done]
                copies[done].wait()
                v = tb[done % N_TB, 0:nr, 0:nc]
                dst[...] = (v if scale == 1.0 else v * scale).astype(BF16)
            if i < len(tiles):
                copies[i].start()

    @pl.when(ti == 0)
    def _():
        start_copies(x_copy, 0, 0)
        start_copies(x_copy, 1, 1)
        start_copies(p_copy, 0, 0)
        start_copies(p_copy, 1, 1)
        load_weights()
        xa_ext[:, 0:CONV_TAIL, :] = jnp.zeros((N_CHUNKS, CONV_TAIL, CW), F32)
        xb_ext[0:POOL_TAIL, :] = jnp.zeros((POOL_TAIL, POOL_WIDTH), F32)
        hst[...] = jnp.zeros((BATCH, LRU_WIDTH), F32)
        wait_slot(xbuf, sem_x, 0)
        norm_in(0, 0)
        dots_xa_ga(0, 0)
        dots_xa_ga(0, 1)

    @pl.when(ti >= 1)
    def _():
        start_copies(o_copy, ti - 1, nslot)

    @pl.when(ti >= 2)
    def _():
        wait_slot(obuf, sem_o, slot)

    wait_slot(pbuf, sem_p, xs)
    wait_slot(xbuf, sem_x, xs1)

    hb = hb_in.at[slot]
    pbf[...] = pbuf[xs].reshape(ROWS, P_DIM).astype(BF16)

    lam = lam_ref[...]
    half_c = (-0.5 * LRU_C) * (jnp.maximum(-lam, 0.0) + jnp.log1p(jnp.exp(-jnp.abs(lam))))

    def conv(c):
        q, c0 = c % 2, c * CW
        acc = 0.5 * cb_ref[:, c0:c0 + CW]
        for k in range(CONV_WIDTH):
            acc = acc + xa_ext[c, k * BATCH:k * BATCH + ROWS, :] * (0.5 * cw_ref[k:k + 1, c0:c0 + CW])
        xcb[q] = acc.astype(BF16)
        xa_ext[c, 0:CONV_TAIL, :] = xa_ext[c, ROWS:ROWS + CONV_TAIL, :]
        return acc

    def dots_gates(c):
        q = c % 2
        for hh, tg in enumerate((TB_G0, TB_G1)):
            l0 = hh * LRU_HEAD_DIM
            tb[tg + q] = _dot(xcb[q, :, l0:l0 + LRU_HEAD_DIM], wax_ref[c * HEADS_PER_CHUNK + hh])

    def gates(c, xc):
        q = c % 2
        for hh, tg in enumerate((TB_G0, TB_G1)):
            l0 = hh * LRU_HEAD_DIM
            ch = slice(c * CW + l0, c * CW + l0 + LRU_HEAD_DIM)
            lanes = slice(l0, l0 + LRU_HEAD_DIM)
            r_tanh = jnp.tanh(tb[tg + q, :, 0:LRU_HEAD_DIM] + 0.5 * ba_ref[:, ch])
            i_gate = _tanh_p1(tb[tg + q, :, LRU_HEAD_DIM:2 * LRU_HEAD_DIM] + 0.5 * bx_ref[:, ch])
            a = jnp.exp(half_c[:, ch] * (r_tanh + 1.0))
            y = 1.0 - a * a
            mult = y * lax.rsqrt(jnp.maximum(y, TINY))
            tb[TB_A + q, :, lanes] = a
            tb[TB_U + q, :, lanes] = mult * (i_gate * xc[:, lanes])

    def scan(c):
        q, c0 = c % 2, c * CW
        h = hst[:, c0:c0 + CW]
        for t in range(TT):
            rows = pl.ds(t * BATCH, BATCH)
            h = tb[TB_A + q, rows, :] * h + tb[TB_U + q, rows, :]
            ga = tb[TB_GA + c, rows, :]
            ya[rows, c0:c0 + CW] = (h * (_tanh_p1(ga) * ga)).astype(BF16)
        hst[:, c0:c0 + CW] = h

    def dots_pool_in():
        xb_ext[POOL_TAIL:POOL_TAIL + ROWS, :] = _dot(hb[...], win_ref[:, OFF_XB:OFF_XB + POOL_WIDTH])
        for half in range(POOL_WIDTH // CW):
            tb[TB_GB + half] = _dot(hb[...], win(OFF_GB, half))

    def pool_windows(g):
        k = POOL_WINDOWS[g]
        lanes = slice(g * POOL_GROUP_DIM, (g + 1) * POOL_GROUP_DIM)
        cur = xb_ext[POOL_TAIL:POOL_TAIL + ROWS, lanes]
        s = cur
        for j in range(1, k):
            s = s + xb_ext[POOL_TAIL - j * BATCH:POOL_TAIL - j * BATCH + ROWS, lanes]
        row = lax.broadcasted_iota(jnp.int32, (ROWS, POOL_GROUP_DIM), 0)
        pos = ti * TT + row // BATCH
        cnt = jnp.minimum(pos + 1, k).astype(F32)
        dbuf[g] = (s / cnt - cur).astype(BF16)

    def pool_out(g):
        lanes = slice(g * POOL_GROUP_DIM, (g + 1) * POOL_GROUP_DIM)
        half, gg = divmod(g, CW // POOL_GROUP_DIM)
        gb = tb[TB_GB + half, :, gg * POOL_GROUP_DIM:(gg + 1) * POOL_GROUP_DIM]
        y = _dot(dbuf[g], pw_ref[g]) * ps_ref[:, lanes]
        yb[:, lanes] = (y * (_tanh_p1(gb) * gb)).astype(BF16)

    def dots_merge_logits(c):
        tb[TB_MA + c] = _dot(hb[...], win(OFF_MA, c))
        tb[TB_MB + c] = _dot(hb[...], win(OFF_MB, c))

    for c in range(N_CHUNKS):
        xc = conv(c)
        dots_gates(c)
        if c + 2 < N_CHUNKS:
            dots_xa_ga(slot, c + 2)
        if c == 0:
            dots_pool_in()
        else:
            dots_merge_logits(c - 1)
        gates(c, xc)
        scan(c)
        pool_windows(c)
        pool_out(c)
    dots_merge_logits(N_CHUNKS - 1)
    xb_ext[0:POOL_TAIL, :] = xb_ext[ROWS:ROWS + POOL_TAIL, :]

    for c in range(N_CHUNKS):
        c0 = c * CW
        pa = _dot(ya[...], plru_ref[:, c0:c0 + CW])
        pb = _dot(yb[...], ppool_ref[:, c0:c0 + CW])
        mbuf[:, c0:c0 + CW] = (_tanh_p1(tb[TB_MA + c]) * pa + _tanh_p1(tb[TB_MB + c]) * pb).astype(BF16)

    for c in range(N_CHUNKS):
        c0 = c * CW
        x1[:, c0:c0 + CW] = x_rows(xs, 0, ROWS, c0, CW) + _dot(mbuf[...], wout_ref[:, c0:c0 + CW])
    norm_in(xs1, nslot)
    start_copies(x_copy, nxt2, xs2)
    start_copies(p_copy, nxt2, xs2)

    for c in range(N_CHUNKS):
        tb[TB_PE + c] = _dot(pbf[...], wpe_ref[:, c * CW:(c + 1) * CW])
    for r0 in range(0, ROWS, NORM_RT):
        hb2[pl.ds(r0, NORM_RT), :] = rmsnorm(x1[pl.ds(r0, NORM_RT), :], pg_ref, BF16)

    for c in range(N_CHUNKS):
        c0 = c * CW
        gate = _dot(hb2[...], wpg_ref[:, c0:c0 + CW])
        x1[:, c0:c0 + CW] = x1[:, c0:c0 + CW] + _tanh_p1(gate) * tb[TB_PE + c]
    dots_xa_ga(nslot, 0)
    dots_xa_ga(nslot, 1)

    for r0 in range(0, ROWS, NORM_RT):
        ov = rmsnorm(x1[pl.ds(r0, NORM_RT), :], fg_ref, F32)
        obuf[slot, pl.ds(r0 // BATCH, NORM_RT // BATCH), :, :] = ov.reshape(NORM_RT // BATCH, BATCH, D_MODEL)

    @pl.when(ti == n_steps - 1)
    def _():
        start_copies(o_copy, ti, slot)
        wait_slot(xbuf, sem_x, xs2)
        wait_slot(pbuf, sem_p, xs1)
        wait_slot(pbuf, sem_p, xs2)
        wait_slot(obuf, sem_o, nslot)
        wait_slot(obuf, sem_o, slot)


def _const_spec(shape):
    zeros = (0,) * len(shape)
    return pl.BlockSpec(shape, lambda i: zeros, pipeline_mode=pl.Buffered(1))


@jax.jit
def kernel(x, p, norm_g, w_in, conv_w, conv_b, lru_w_a, lru_b_a, lru_w_x, lru_b_x, lru_lambda,
           pool_w, pool_scale, w_proj_lru, w_proj_pool, w_out, ple_norm_g, w_ple_gate,
           w_ple_proj, final_g):
    assert x.shape == (BATCH, SEQ, D_MODEL) and p.shape == (1, BATCH, SEQ, P_DIM)
    row = lambda v: v.reshape(1, -1)
    operands = [
        x, p, row(norm_g[0]), w_in, conv_w[0], row(conv_b[0]), lru_w_a, row(lru_b_a[0]), lru_w_x,
        row(lru_b_x[0]), row(lru_lambda[0]), pool_w, row(pool_scale[0]), w_proj_lru, w_proj_pool,
        w_out, row(ple_norm_g[0]), w_ple_gate, w_ple_proj, row(final_g),
    ]
    in_specs = [pl.BlockSpec(memory_space=pl.ANY) if v.ndim > 2 else _const_spec(v.shape)
                for v in operands]
    scratch = [
        pltpu.VMEM((X_SLOTS, TT, BATCH, D_MODEL), F32),
        pltpu.VMEM((X_SLOTS, TT, BATCH, P_DIM), F32),
        pltpu.VMEM((2, TT, BATCH, D_MODEL), F32),
        pltpu.SemaphoreType.DMA((X_SLOTS,)),
        pltpu.SemaphoreType.DMA((X_SLOTS,)),
        pltpu.SemaphoreType.DMA((2,)),
        pltpu.SemaphoreType.DMA((N_TB,)),
        pltpu.VMEM((D_MODEL, IN_COLS), BF16),
        pltpu.VMEM((LRU_HEADS, LRU_HEAD_DIM, 2 * LRU_HEAD_DIM), BF16),
        pltpu.VMEM((POOL_GROUPS, POOL_GROUP_DIM, POOL_GROUP_DIM), BF16),
        pltpu.VMEM((LRU_WIDTH, D_MODEL), BF16),
        pltpu.VMEM((POOL_WIDTH, D_MODEL), BF16),
        pltpu.VMEM((D_MODEL, D_MODEL), BF16),
        pltpu.VMEM((D_MODEL, D_MODEL), BF16),
        pltpu.VMEM((P_DIM, D_MODEL), BF16),
        pltpu.VMEM((2, ROWS, D_MODEL), BF16),
        pltpu.VMEM((ROWS, D_MODEL), BF16),
        pltpu.VMEM((ROWS, P_DIM), BF16),
        pltpu.VMEM((N_CHUNKS, ROWS + CONV_TAIL, CW), F32),
        pltpu.VMEM((ROWS + POOL_TAIL, POOL_WIDTH), F32),
        pltpu.VMEM((BATCH, LRU_WIDTH), F32),
        pltpu.VMEM((N_TB, ROWS, CW), F32),
        pltpu.VMEM((2, ROWS, CW), BF16),
        pltpu.VMEM((ROWS, LRU_WIDTH), BF16),
        pltpu.VMEM((ROWS, POOL_WIDTH), BF16),
        pltpu.VMEM((POOL_GROUPS, ROWS, POOL_GROUP_DIM), BF16),
        pltpu.VMEM((ROWS, D_MODEL), BF16),
        pltpu.VMEM((ROWS, D_MODEL), F32),
    ]
    scratch_vmem_bytes = sum(math.prod(s.shape) * jnp.dtype(s.dtype).itemsize
                             for s in scratch if s.memory_space == pltpu.VMEM)
    return pl.pallas_call(
        _block_kernel,
        grid=(SEQ // TT,),
        in_specs=in_specs,
        out_specs=pl.BlockSpec(memory_space=pl.ANY),
        out_shape=jax.ShapeDtypeStruct((BATCH, SEQ, D_MODEL), F32),
        scratch_shapes=scratch,
        compiler_params=pltpu.CompilerParams(
            dimension_semantics=("arbitrary",),
            vmem_limit_bytes=scratch_vmem_bytes + COMPILER_VMEM_ALLOWANCE_BYTES,
        ),
        name="rglru_pool_block",
    )(*operands)
```

```python
import math

import jax
import jax.numpy as jnp
from jax import lax
from jax.experimental import pallas as pl
from jax.experimental.pallas import tpu as pltpu

D_MODEL = 1024
BATCH = 16
SEQ = 2048
P_DIM = 256
LRU_WIDTH = 1024
LRU_HEADS = 8
LRU_HEAD_DIM = 128
CONV_WIDTH = 4
LRU_C = 8.0
POOL_WIDTH = 512
POOL_WINDOWS = (2, 4, 8, 16)
POOL_GROUPS = 4
POOL_GROUP_DIM = 128
MAX_WIN = 16
EPS = 1e-6

OFF_XA = 0
OFF_GA = LRU_WIDTH
OFF_XB = 2 * LRU_WIDTH
OFF_GB = OFF_XB + POOL_WIDTH
OFF_MA = OFF_GB + POOL_WIDTH
OFF_MB = OFF_MA + D_MODEL
IN_COLS = OFF_MB + D_MODEL

TT = 32
ROWS = TT * BATCH
CW = 256
N_CHUNKS = D_MODEL // CW
HEADS_PER_CHUNK = CW // LRU_HEAD_DIM
NORM_RT = ROWS
CONV_TAIL = (CONV_WIDTH - 1) * BATCH
POOL_TAIL = MAX_WIN * BATCH
X_SLOTS = 3
COMPILER_VMEM_ALLOWANCE_BYTES = 4 * 1024 * 1024

TB_GA = 0
TB_G0, TB_G1, TB_A, TB_U = 6, 8, 10, 12
TB_MA, TB_MB = 14, 18
TB_GB = 22
N_TB = 24
TB_PE = 6

F32 = jnp.float32
BF16 = jnp.bfloat16
TINY = 1e-30


def _tanh_p1(vh):
    return jnp.tanh(vh) + 1.0


def _dot(a, b):
    return jnp.dot(a, b, preferred_element_type=F32)


def _block_kernel(x_hbm, p_hbm, ng_ref, win_hbm, cw_ref, cb_ref, wa_hbm, ba_ref, wx_hbm, bx_ref,
                  lam_ref, pw_hbm, ps_ref, plru_hbm, ppool_hbm, wout_hbm, pg_ref, wpg_hbm,
                  wpe_hbm, fg_ref,
                  o_hbm,
                  xbuf, pbuf, obuf, sem_x, sem_p, sem_o, sem_w,
                  win_ref, wax_ref, pw_ref, plru_ref, ppool_ref, wout_ref, wpg_ref, wpe_ref,
                  hb_in, hb2, pbf, xa_ext, xb_ext, hst, tb, xcb, ya, yb, dbuf, mbuf, x1):
    ti = pl.program_id(0)
    n_steps = pl.num_programs(0)
    slot = ti % 2
    nxt = jnp.minimum(ti + 1, n_steps - 1)
    nxt2 = jnp.minimum(ti + 2, n_steps - 1)
    nslot = 1 - slot
    xs, xs1, xs2 = ti % X_SLOTS, (ti + 1) % X_SLOTS, (ti + 2) % X_SLOTS

    def x_copy(step, sl, b):
        return pltpu.make_async_copy(x_hbm.at[b, pl.ds(step * TT, TT), :], xbuf.at[sl, :, b, :],
                                     sem_x.at[sl])

    def p_copy(step, sl, b):
        return pltpu.make_async_copy(p_hbm.at[0, b, pl.ds(step * TT, TT), :], pbuf.at[sl, :, b, :],
                                     sem_p.at[sl])

    def o_copy(step, sl, b):
        return pltpu.make_async_copy(obuf.at[sl, :, b, :], o_hbm.at[b, pl.ds(step * TT, TT), :],
                                     sem_o.at[sl])

    def start_copies(copy, step, sl):
        for b in range(BATCH):
            copy(step, sl, b).start()

    def wait_slot(buf, sem, sl):
        pltpu.make_async_copy(buf.at[sl], buf.at[sl], sem.at[sl]).wait()

    def x_rows(sl, r0, nrows, c0=0, ncols=D_MODEL):
        t0, nt = r0 // BATCH, nrows // BATCH
        return xbuf[sl, pl.ds(t0, nt), :, c0:c0 + ncols].reshape(nrows, ncols)

    def rmsnorm(xv, g_ref, cast):
        ms = jnp.mean(xv * xv, axis=-1, keepdims=True)
        return ((xv * lax.rsqrt(ms + EPS)) * g_ref[...]).astype(cast)

    def win(off, c):
        return win_ref[:, off + c * CW:off + (c + 1) * CW]

    def norm_in(xsl, sl):
        for r0 in range(0, ROWS, NORM_RT):
            hb_in[sl, pl.ds(r0, NORM_RT), :] = rmsnorm(x_rows(xsl, r0, NORM_RT), ng_ref, BF16)

    def dots_xa_ga(sl, c):
        xa_ext[c, CONV_TAIL:CONV_TAIL + ROWS, :] = _dot(hb_in[sl], win(OFF_XA, c))
        tb[TB_GA + c] = _dot(hb_in[sl], win(OFF_GA, c))

    def weight_tiles():
        tiles = []

        def add(src, dst, nrows, ncols, scale):
            for r0 in range(0, nrows, ROWS):
                nr = min(ROWS, nrows - r0)
                for c0 in range(0, ncols, CW):
                    tiles.append((src.at[0, pl.ds(r0, nr), pl.ds(c0, CW)], nr, CW,
                                  dst.at[pl.ds(r0, nr), pl.ds(c0, CW)],
                                  scale(c0) if callable(scale) else scale))

        halved_in = lambda c0: 0.5 if (OFF_GA <= c0 < OFF_XB or c0 >= OFF_GB) else 1.0
        add(win_hbm, win_ref, D_MODEL, IN_COLS, halved_in)
        add(plru_hbm, plru_ref, LRU_WIDTH, D_MODEL, 0.5)
        add(ppool_hbm, ppool_ref, POOL_WIDTH, D_MODEL, 0.5)
        add(wout_hbm, wout_ref, D_MODEL, D_MODEL, 1.0)
        add(wpg_hbm, wpg_ref, D_MODEL, D_MODEL, 0.5)
        add(wpe_hbm, wpe_ref, P_DIM, D_MODEL, 0.5)
        for h in range(LRU_HEADS):
            for part, src in enumerate((wa_hbm, wx_hbm)):
                tiles.append((src.at[0, h], LRU_HEAD_DIM, LRU_HEAD_DIM,
                              wax_ref.at[h, :, pl.ds(part * LRU_HEAD_DIM, LRU_HEAD_DIM)], 0.5))
        for g in range(POOL_GROUPS):
            tiles.append((pw_hbm.at[0, g], POOL_GROUP_DIM, POOL_GROUP_DIM, pw_ref.at[g], 1.0))
        return tiles

    def load_weights():
        tiles = weight_tiles()
        copies = [pltpu.make_async_copy(src, tb.at[i % N_TB, pl.ds(0, nr), pl.ds(0, nc)], sem_w.at[i % N_TB])
                  for i, (src, nr, nc, _, _) in enumerate(tiles)]
        for i in range(len(tiles) + N_TB):
            done = i - N_TB
            if done >= 0:
                _, nr, nc, dst, scale = tiles[done]
                copies[done].wait()
                v = tb[done % N_TB, 0:nr, 0:nc]
                dst[...] = (v if scale == 1.0 else v * scale).astype(BF16)
            if i < len(tiles):
                copies[i].start()

    @pl.when(ti == 0)
    def _():
        start_copies(x_copy, 0, 0)
        start_copies(x_copy, 1, 1)
        start_copies(p_copy, 0, 0)
        start_copies(p_copy, 1, 1)
        load_weights()
        xa_ext[:, 0:CONV_TAIL, :] = jnp.zeros((N_CHUNKS, CONV_TAIL, CW), F32)
        xb_ext[0:POOL_TAIL, :] = jnp.zeros((POOL_TAIL, POOL_WIDTH), F32)
        hst[...] = jnp.zeros((BATCH, LRU_WIDTH), F32)
        wait_slot(xbuf, sem_x, 0)
        norm_in(0, 0)
        dots_xa_ga(0, 0)
        dots_xa_ga(0, 1)

    @pl.when(ti >= 1)
    def _():
        start_copies(o_copy, ti - 1, nslot)

    @pl.when(ti >= 2)
    def _():
        wait_slot(obuf, sem_o, slot)

    wait_slot(pbuf, sem_p, xs)
    wait_slot(xbuf, sem_x, xs1)

    hb = hb_in.at[slot]
    pbf[...] = pbuf[xs].reshape(ROWS, P_DIM).astype(BF16)

    lam = lam_ref[...]
    half_c = (-0.5 * LRU_C) * (jnp.maximum(-lam, 0.0) + jnp.log1p(jnp.exp(-jnp.abs(lam))))

    def conv(c):
        q, c0 = c % 2, c * CW
        acc = cb_ref[:, c0:c0 + CW]
        for k in range(CONV_WIDTH):
            acc = acc + xa_ext[c, k * BATCH:k * BATCH + ROWS, :] * cw_ref[k:k + 1, c0:c0 + CW]
        xcb[q] = acc.astype(BF16)
        xa_ext[c, 0:CONV_TAIL, :] = xa_ext[c, ROWS:ROWS + CONV_TAIL, :]
        return acc

    def dots_gates(c):
        q = c % 2
        for hh, tg in enumerate((TB_G0, TB_G1)):
            l0 = hh * LRU_HEAD_DIM
            tb[tg + q] = _dot(xcb[q, :, l0:l0 + LRU_HEAD_DIM], wax_ref[c * HEADS_PER_CHUNK + hh])

    def gates(c, xc):
        q = c % 2
        for hh, tg in enumerate((TB_G0, TB_G1)):
            l0 = hh * LRU_HEAD_DIM
            ch = slice(c * CW + l0, c * CW + l0 + LRU_HEAD_DIM)
            lanes = slice(l0, l0 + LRU_HEAD_DIM)
            r_tanh = jnp.tanh(tb[tg + q, :, 0:LRU_HEAD_DIM] + 0.5 * ba_ref[:, ch])
            i_gate = 0.5 * jnp.tanh(tb[tg + q, :, LRU_HEAD_DIM:2 * LRU_HEAD_DIM] + 0.5 * bx_ref[:, ch]) + 0.5
            a = jnp.exp(half_c[:, ch] * (r_tanh + 1.0))
            y = 1.0 - a * a
            mult = y * lax.rsqrt(jnp.maximum(y, TINY))
            tb[TB_A + q, :, lanes] = a
            tb[TB_U + q, :, lanes] = mult * (i_gate * xc[:, lanes])

    def scan(c):
        q, c0 = c % 2, c * CW
        h = hst[:, c0:c0 + CW]
        for t in range(TT):
            rows = pl.ds(t * BATCH, BATCH)
            h = tb[TB_A + q, rows, :] * h + tb[TB_U + q, rows, :]
            ga = tb[TB_GA + c, rows, :]
            ya[rows, c0:c0 + CW] = (h * (_tanh_p1(ga) * ga)).astype(BF16)
        hst[:, c0:c0 + CW] = h

    def dots_pool_in():
        xb_ext[POOL_TAIL:POOL_TAIL + ROWS, :] = _dot(hb[...], win_ref[:, OFF_XB:OFF_XB + POOL_WIDTH])
        for half in range(POOL_WIDTH // CW):
            tb[TB_GB + half] = _dot(hb[...], win(OFF_GB, half))

    def pool_windows(g):
        k = POOL_WINDOWS[g]
        lanes = slice(g * POOL_GROUP_DIM, (g + 1) * POOL_GROUP_DIM)
        cur = xb_ext[POOL_TAIL:POOL_TAIL + ROWS, lanes]
        s = cur
        for j in range(1, k):
            s = s + xb_ext[POOL_TAIL - j * BATCH:POOL_TAIL - j * BATCH + ROWS, lanes]
        row = lax.broadcasted_iota(jnp.int32, (ROWS, POOL_GROUP_DIM), 0)
        pos = ti * TT + row // BATCH
        cnt = jnp.minimum(pos + 1, k).astype(F32)
        dbuf[g] = (s / cnt - cur).astype(BF16)

    def pool_out(g):
        lanes = slice(g * POOL_GROUP_DIM, (g + 1) * POOL_GROUP_DIM)
        half, gg = divmod(g, CW // POOL_GROUP_DIM)
        gb = tb[TB_GB + half, :, gg * POOL_GROUP_DIM:(gg + 1) * POOL_GROUP_DIM]
        y = _dot(dbuf[g], pw_ref[g]) * ps_ref[:, lanes]
        yb[:, lanes] = (y * (_tanh_p1(gb) * gb)).astype(BF16)

    def dots_merge_logits(c):
        tb[TB_MA + c] = _dot(hb[...], win(OFF_MA, c))
        tb[TB_MB + c] = _dot(hb[...], win(OFF_MB, c))

    for c in range(N_CHUNKS):
        xc = conv(c)
        dots_gates(c)
        if c + 2 < N_CHUNKS:
            dots_xa_ga(slot, c + 2)
        if c == 0:
            dots_pool_in()
        else:
            dots_merge_logits(c - 1)
        gates(c, xc)
        scan(c)
        pool_windows(c)
        pool_out(c)
    dots_merge_logits(N_CHUNKS - 1)
    xb_ext[0:POOL_TAIL, :] = xb_ext[ROWS:ROWS + POOL_TAIL, :]

    for c in range(N_CHUNKS):
        c0 = c * CW
        pa = _dot(ya[...], plru_ref[:, c0:c0 + CW])
        pb = _dot(yb[...], ppool_ref[:, c0:c0 + CW])
        mbuf[:, c0:c0 + CW] = (_tanh_p1(tb[TB_MA + c]) * pa + _tanh_p1(tb[TB_MB + c]) * pb).astype(BF16)

    for c in range(N_CHUNKS):
        c0 = c * CW
        x1[:, c0:c0 + CW] = x_rows(xs, 0, ROWS, c0, CW) + _dot(mbuf[...], wout_ref[:, c0:c0 + CW])
    norm_in(xs1, nslot)
    start_copies(x_copy, nxt2, xs2)
    start_copies(p_copy, nxt2, xs2)

    for c in range(N_CHUNKS):
        tb[TB_PE + c] = _dot(pbf[...], wpe_ref[:, c * CW:(c + 1) * CW])
    for r0 in range(0, ROWS, NORM_RT):
        hb2[pl.ds(r0, NORM_RT), :] = rmsnorm(x1[pl.ds(r0, NORM_RT), :], pg_ref, BF16)

    for c in range(N_CHUNKS):
        c0 = c * CW
        gate = _dot(hb2[...], wpg_ref[:, c0:c0 + CW])
        x1[:, c0:c0 + CW] = x1[:, c0:c0 + CW] + _tanh_p1(gate) * tb[TB_PE + c]
    dots_xa_ga(nslot, 0)
    dots_xa_ga(nslot, 1)

    for r0 in range(0, ROWS, NORM_RT):
        ov = rmsnorm(x1[pl.ds(r0, NORM_RT), :], fg_ref, F32)
        obuf[slot, pl.ds(r0 // BATCH, NORM_RT // BATCH), :, :] = ov.reshape(NORM_RT // BATCH, BATCH, D_MODEL)

    @pl.when(ti == n_steps - 1)
    def _():
        start_copies(o_copy, ti, slot)
        wait_slot(xbuf, sem_x, xs2)
        wait_slot(pbuf, sem_p, xs1)
        wait_slot(pbuf, sem_p, xs2)
        wait_slot(obuf, sem_o, nslot)
        wait_slot(obuf, sem_o, slot)


def _const_spec(shape):
    zeros = (0,) * len(shape)
    return pl.BlockSpec(shape, lambda i: zeros, pipeline_mode=pl.Buffered(1))


@jax.jit
def kernel(x, p, norm_g, w_in, conv_w, conv_b, lru_w_a, lru_b_a, lru_w_x, lru_b_x, lru_lambda,
           pool_w, pool_scale, w_proj_lru, w_proj_pool, w_out, ple_norm_g, w_ple_gate,
           w_ple_proj, final_g):
    assert x.shape == (BATCH, SEQ, D_MODEL) and p.shape == (1, BATCH, SEQ, P_DIM)
    row = lambda v: v.reshape(1, -1)
    operands = [
        x, p, row(norm_g[0]), w_in, conv_w[0], row(conv_b[0]), lru_w_a, row(lru_b_a[0]), lru_w_x,
        row(lru_b_x[0]), row(lru_lambda[0]), pool_w, row(pool_scale[0]), w_proj_lru, w_proj_pool,
        w_out, row(ple_norm_g[0]), w_ple_gate, w_ple_proj, row(final_g),
    ]
    in_specs = [pl.BlockSpec(memory_space=pl.ANY) if v.ndim > 2 else _const_spec(v.shape)
                for v in operands]
    scratch = [
        pltpu.VMEM((X_SLOTS, TT, BATCH, D_MODEL), F32),
        pltpu.VMEM((X_SLOTS, TT, BATCH, P_DIM), F32),
        pltpu.VMEM((2, TT, BATCH, D_MODEL), F32),
        pltpu.SemaphoreType.DMA((X_SLOTS,)),
        pltpu.SemaphoreType.DMA((X_SLOTS,)),
        pltpu.SemaphoreType.DMA((2,)),
        pltpu.SemaphoreType.DMA((N_TB,)),
        pltpu.VMEM((D_MODEL, IN_COLS), BF16),
        pltpu.VMEM((LRU_HEADS, LRU_HEAD_DIM, 2 * LRU_HEAD_DIM), BF16),
        pltpu.VMEM((POOL_GROUPS, POOL_GROUP_DIM, POOL_GROUP_DIM), BF16),
        pltpu.VMEM((LRU_WIDTH, D_MODEL), BF16),
        pltpu.VMEM((POOL_WIDTH, D_MODEL), BF16),
        pltpu.VMEM((D_MODEL, D_MODEL), BF16),
        pltpu.VMEM((D_MODEL, D_MODEL), BF16),
        pltpu.VMEM((P_DIM, D_MODEL), BF16),
        pltpu.VMEM((2, ROWS, D_MODEL), BF16),
        pltpu.VMEM((ROWS, D_MODEL), BF16),
        pltpu.VMEM((ROWS, P_DIM), BF16),
        pltpu.VMEM((N_CHUNKS, ROWS + CONV_TAIL, CW), F32),
        pltpu.VMEM((ROWS + POOL_TAIL, POOL_WIDTH), F32),
        pltpu.VMEM((BATCH, LRU_WIDTH), F32),
        pltpu.VMEM((N_TB, ROWS, CW), F32),
        pltpu.VMEM((2, ROWS, CW), BF16),
        pltpu.VMEM((ROWS, LRU_WIDTH), BF16),
        pltpu.VMEM((ROWS, POOL_WIDTH), BF16),
        pltpu.VMEM((POOL_GROUPS, ROWS, POOL_GROUP_DIM), BF16),
        pltpu.VMEM((ROWS, D_MODEL), BF16),
        pltpu.VMEM((ROWS, D_MODEL), F32),
    ]
    scratch_vmem_bytes = sum(math.prod(s.shape) * jnp.dtype(s.dtype).itemsize
                             for s in scratch if s.memory_space == pltpu.VMEM)
    return pl.pallas_call(
        _block_kernel,
        grid=(SEQ // TT,),
        in_specs=in_specs,
        out_specs=pl.BlockSpec(memory_space=pl.ANY),
        out_shape=jax.ShapeDtypeStruct((BATCH, SEQ, D_MODEL), F32),
        scratch_shapes=scratch,
        compiler_params=pltpu.CompilerParams(
            dimension_semantics=("arbitrary",),
            vmem_limit_bytes=scratch_vmem_bytes + COMPILER_VMEM_ALLOWANCE_BYTES,
        ),
        name="rglru_pool_block",
    )(*operands)
```

```python
import math

import jax
import jax.numpy as jnp
from jax import lax
from jax.experimental import pallas as pl
from jax.experimental.pallas import tpu as pltpu

D_MODEL = 1024
BATCH = 16
SEQ = 2048
P_DIM = 256
LRU_WIDTH = 1024
LRU_HEADS = 8
LRU_HEAD_DIM = 128
CONV_WIDTH = 4
LRU_C = 8.0
POOL_WIDTH = 512
POOL_WINDOWS = (2, 4, 8, 16)
POOL_GROUPS = 4
POOL_GROUP_DIM = 128
MAX_WIN = 16
EPS = 1e-6

OFF_XA = 0
OFF_GA = LRU_WIDTH
OFF_XB = 2 * LRU_WIDTH
OFF_GB = OFF_XB + POOL_WIDTH
OFF_MA = OFF_GB + POOL_WIDTH
OFF_MB = OFF_MA + D_MODEL
IN_COLS = OFF_MB + D_MODEL

TT = 32
ROWS = TT * BATCH
CW = 256
N_CHUNKS = D_MODEL // CW
HEADS_PER_CHUNK = CW // LRU_HEAD_DIM
CONV_TAIL = (CONV_WIDTH - 1) * BATCH
POOL_TAIL = MAX_WIN * BATCH
X_SLOTS = 3
COMPILER_VMEM_ALLOWANCE_BYTES = 4 * 1024 * 1024

TB_GA = 0
TB_G0, TB_G1, TB_A, TB_U = 6, 8, 10, 12
TB_MA, TB_MB = 14, 18
TB_GB = 22
N_TB = 24
TB_PE = 6

F32 = jnp.float32
BF16 = jnp.bfloat16
TINY = 1e-30


def _tanh_p1(vh):
    return jnp.tanh(vh) + 1.0


def _dot(a, b):
    return jnp.dot(a, b, preferred_element_type=F32)


def _block_kernel(x_hbm, p_hbm, ng_ref, win_hbm, cw_ref, cb_ref, wa_hbm, ba_ref, wx_hbm, bx_ref,
                  lam_ref, pw_hbm, ps_ref, plru_hbm, ppool_hbm, wout_hbm, pg_ref, wpg_hbm,
                  wpe_hbm, fg_ref,
                  o_hbm,
                  xbuf, pbuf, obuf, sem_x, sem_p, sem_o, sem_w,
                  win_ref, wax_ref, pw_ref, plru_ref, ppool_ref, wout_ref, wpg_ref, wpe_ref,
                  hb_in, hb2, pbf, xa_ext, xb_ext, hst, tb, xcb, ya, yb, dbuf, mbuf, x1):
    ti = pl.program_id(0)
    n_steps = pl.num_programs(0)
    slot = ti % 2
    nxt = jnp.minimum(ti + 1, n_steps - 1)
    nxt2 = jnp.minimum(ti + 2, n_steps - 1)
    nslot = 1 - slot
    xs, xs1, xs2 = ti % X_SLOTS, (ti + 1) % X_SLOTS, (ti + 2) % X_SLOTS

    def x_copy(step, sl, b):
        return pltpu.make_async_copy(x_hbm.at[b, pl.ds(step * TT, TT), :], xbuf.at[sl, :, b, :],
                                     sem_x.at[sl])

    def p_copy(step, sl, b):
        return pltpu.make_async_copy(p_hbm.at[0, b, pl.ds(step * TT, TT), :], pbuf.at[sl, :, b, :],
                                     sem_p.at[sl])

    def o_copy(step, sl, b):
        return pltpu.make_async_copy(obuf.at[sl, :, b, :], o_hbm.at[b, pl.ds(step * TT, TT), :],
                                     sem_o.at[sl])

    def start_copies(copy, step, sl):
        for b in range(BATCH):
            copy(step, sl, b).start()

    def wait_slot(buf, sem, sl):
        pltpu.make_async_copy(buf.at[sl], buf.at[sl], sem.at[sl]).wait()

    def x_rows(sl, r0, nrows, c0=0, ncols=D_MODEL):
        t0, nt = r0 // BATCH, nrows // BATCH
        return xbuf[sl, pl.ds(t0, nt), :, c0:c0 + ncols].reshape(nrows, ncols)

    def rmsnorm(xv, g_ref, cast):
        ms = jnp.mean(xv * xv, axis=-1, keepdims=True)
        return ((xv * lax.rsqrt(ms + EPS)) * g_ref[...]).astype(cast)

    def win(off, c):
        return win_ref[:, off + c * CW:off + (c + 1) * CW]

    def norm_in(xsl, sl):
        hb_in[sl] = rmsnorm(x_rows(xsl, 0, ROWS), ng_ref, BF16)

    def dots_xa_ga(sl, c):
        xa_ext[c, CONV_TAIL:CONV_TAIL + ROWS, :] = _dot(hb_in[sl], win(OFF_XA, c))
        tb[TB_GA + c] = _dot(hb_in[sl], win(OFF_GA, c))

    def weight_tiles():
        tiles = []

        def add(src, dst, nrows, ncols, scale):
            for r0 in range(0, nrows, ROWS):
                nr = min(ROWS, nrows - r0)
                for c0 in range(0, ncols, CW):
                    tiles.append((src.at[0, pl.ds(r0, nr), pl.ds(c0, CW)], nr, CW,
                                  dst.at[pl.ds(r0, nr), pl.ds(c0, CW)],
                                  scale(c0) if callable(scale) else scale))

        halved_in = lambda c0: 0.5 if (OFF_GA <= c0 < OFF_XB or c0 >= OFF_GB) else 1.0
        add(win_hbm, win_ref, D_MODEL, IN_COLS, halved_in)
        add(plru_hbm, plru_ref, LRU_WIDTH, D_MODEL, 0.5)
        add(ppool_hbm, ppool_ref, POOL_WIDTH, D_MODEL, 0.5)
        add(wout_hbm, wout_ref, D_MODEL, D_MODEL, 1.0)
        add(wpg_hbm, wpg_ref, D_MODEL, D_MODEL, 0.5)
        add(wpe_hbm, wpe_ref, P_DIM, D_MODEL, 0.5)
        for h in range(LRU_HEADS):
            for part, src in enumerate((wa_hbm, wx_hbm)):
                tiles.append((src.at[0, h], LRU_HEAD_DIM, LRU_HEAD_DIM,
                              wax_ref.at[h, :, pl.ds(part * LRU_HEAD_DIM, LRU_HEAD_DIM)], 0.5))
        for g in range(POOL_GROUPS):
            tiles.append((pw_hbm.at[0, g], POOL_GROUP_DIM, POOL_GROUP_DIM, pw_ref.at[g], 1.0))
        return tiles

    def load_weights():
        tiles = weight_tiles()
        copies = [pltpu.make_async_copy(src, tb.at[i % N_TB, pl.ds(0, nr), pl.ds(0, nc)], sem_w.at[i % N_TB])
                  for i, (src, nr, nc, _, _) in enumerate(tiles)]
        for i in range(len(tiles) + N_TB):
            done = i - N_TB
            if done >= 0:
                _, nr, nc, dst, scale = tiles[done]
                copies[done].wait()
                v = tb[done % N_TB, 0:nr, 0:nc]
                dst[...] = (v if scale == 1.0 else v * scale).astype(BF16)
            if i < len(tiles):
                copies[i].start()

    @pl.when(ti == 0)
    def _():
        start_copies(x_copy, 0, 0)
        start_copies(x_copy, 1, 1)
        start_copies(p_copy, 0, 0)
        start_copies(p_copy, 1, 1)
        load_weights()
        xa_ext[:, 0:CONV_TAIL, :] = jnp.zeros((N_CHUNKS, CONV_TAIL, CW), F32)
        xb_ext[0:POOL_TAIL, :] = jnp.zeros((POOL_TAIL, POOL_WIDTH), F32)
        hst[...] = jnp.zeros((BATCH, LRU_WIDTH), F32)
        wait_slot(xbuf, sem_x, 0)
        norm_in(0, 0)
        dots_xa_ga(0, 0)
        dots_xa_ga(0, 1)

    @pl.when(ti >= 1)
    def _():
        start_copies(o_copy, ti - 1, nslot)

    @pl.when(ti >= 2)
    def _():
        wait_slot(obuf, sem_o, slot)

    wait_slot(pbuf, sem_p, xs)
    wait_slot(xbuf, sem_x, xs1)

    hb = hb_in.at[slot]
    pbf[...] = pbuf[xs].reshape(ROWS, P_DIM).astype(BF16)

    lam = lam_ref[...]
    half_c = (-0.5 * LRU_C) * (jnp.maximum(-lam, 0.0) + jnp.log1p(jnp.exp(-jnp.abs(lam))))

    def conv(c):
        q, c0 = c % 2, c * CW
        acc = cb_ref[:, c0:c0 + CW]
        for k in range(CONV_WIDTH):
            acc = acc + xa_ext[c, k * BATCH:k * BATCH + ROWS, :] * cw_ref[k:k + 1, c0:c0 + CW]
        xcb[q] = acc.astype(BF16)
        xa_ext[c, 0:CONV_TAIL, :] = xa_ext[c, ROWS:ROWS + CONV_TAIL, :]
        return acc

    def dots_gates(c):
        q = c % 2
        for hh, tg in enumerate((TB_G0, TB_G1)):
            l0 = hh * LRU_HEAD_DIM
            tb[tg + q] = _dot(xcb[q, :, l0:l0 + LRU_HEAD_DIM], wax_ref[c * HEADS_PER_CHUNK + hh])

    def gates(c, xc):
        q = c % 2
        for hh, tg in enumerate((TB_G0, TB_G1)):
            l0 = hh * LRU_HEAD_DIM
            ch = slice(c * CW + l0, c * CW + l0 + LRU_HEAD_DIM)
            lanes = slice(l0, l0 + LRU_HEAD_DIM)
            r_tanh = jnp.tanh(tb[tg + q, :, 0:LRU_HEAD_DIM] + 0.5 * ba_ref[:, ch])
            i_gate = 0.5 * jnp.tanh(tb[tg + q, :, LRU_HEAD_DIM:2 * LRU_HEAD_DIM] + 0.5 * bx_ref[:, ch]) + 0.5
            a = jnp.exp(half_c[:, ch] * (r_tanh + 1.0))
            y = 1.0 - a * a
            mult = y * lax.rsqrt(jnp.maximum(y, TINY))
            tb[TB_A + q, :, lanes] = a
            tb[TB_U + q, :, lanes] = mult * (i_gate * xc[:, lanes])

    def scan(c):
        q, c0 = c % 2, c * CW
        h = hst[:, c0:c0 + CW]
        for t in range(TT):
            rows = pl.ds(t * BATCH, BATCH)
            h = tb[TB_A + q, rows, :] * h + tb[TB_U + q, rows, :]
            ga = tb[TB_GA + c, rows, :]
            ya[rows, c0:c0 + CW] = (h * (_tanh_p1(ga) * ga)).astype(BF16)
        hst[:, c0:c0 + CW] = h

    def dots_pool_in():
        xb_ext[POOL_TAIL:POOL_TAIL + ROWS, :] = _dot(hb[...], win_ref[:, OFF_XB:OFF_XB + POOL_WIDTH])
        for half in range(POOL_WIDTH // CW):
            tb[TB_GB + half] = _dot(hb[...], win(OFF_GB, half))

    def pool_windows(g):
        k = POOL_WINDOWS[g]
        lanes = slice(g * POOL_GROUP_DIM, (g + 1) * POOL_GROUP_DIM)
        cur = xb_ext[POOL_TAIL:POOL_TAIL + ROWS, lanes]
        s = cur
        for j in range(1, k):
            s = s + xb_ext[POOL_TAIL - j * BATCH:POOL_TAIL - j * BATCH + ROWS, lanes]
        row = lax.broadcasted_iota(jnp.int32, (ROWS, POOL_GROUP_DIM), 0)
        pos = ti * TT + row // BATCH
        cnt = jnp.minimum(pos + 1, k).astype(F32)
        dbuf[g] = (s / cnt - cur).astype(BF16)

    def pool_out(g):
        lanes = slice(g * POOL_GROUP_DIM, (g + 1) * POOL_GROUP_DIM)
        half, gg = divmod(g, CW // POOL_GROUP_DIM)
        gb = tb[TB_GB + half, :, gg * POOL_GROUP_DIM:(gg + 1) * POOL_GROUP_DIM]
        y = _dot(dbuf[g], pw_ref[g]) * ps_ref[:, lanes]
        yb[:, lanes] = (y * (_tanh_p1(gb) * gb)).astype(BF16)

    def dots_merge_logits(c):
        tb[TB_MA + c] = _dot(hb[...], win(OFF_MA, c))
        tb[TB_MB + c] = _dot(hb[...], win(OFF_MB, c))

    for c in range(N_CHUNKS):
        xc = conv(c)
        dots_gates(c)
        if c + 2 < N_CHUNKS:
            dots_xa_ga(slot, c + 2)
        if c == 0:
            dots_pool_in()
        else:
            dots_merge_logits(c - 1)
        gates(c, xc)
        scan(c)
        pool_windows(c)
        pool_out(c)
    dots_merge_logits(N_CHUNKS - 1)
    xb_ext[0:POOL_TAIL, :] = xb_ext[ROWS:ROWS + POOL_TAIL, :]

    for c in range(N_CHUNKS):
        c0 = c * CW
        pa = _dot(ya[...], plru_ref[:, c0:c0 + CW])
        pb = _dot(yb[...], ppool_ref[:, c0:c0 + CW])
        mbuf[:, c0:c0 + CW] = (_tanh_p1(tb[TB_MA + c]) * pa + _tanh_p1(tb[TB_MB + c]) * pb).astype(BF16)

    for c in range(N_CHUNKS):
        c0 = c * CW
        x1[:, c0:c0 + CW] = x_rows(xs, 0, ROWS, c0, CW) + _dot(mbuf[...], wout_ref[:, c0:c0 + CW])
    norm_in(xs1, nslot)
    start_copies(x_copy, nxt2, xs2)
    start_copies(p_copy, nxt2, xs2)

    for c in range(N_CHUNKS):
        tb[TB_PE + c] = _dot(pbf[...], wpe_ref[:, c * CW:(c + 1) * CW])
    hb2[...] = rmsnorm(x1[...], pg_ref, BF16)

    for c in range(N_CHUNKS):
        c0 = c * CW
        gate = _dot(hb2[...], wpg_ref[:, c0:c0 + CW])
        x1[:, c0:c0 + CW] = x1[:, c0:c0 + CW] + _tanh_p1(gate) * tb[TB_PE + c]
    dots_xa_ga(nslot, 0)
    dots_xa_ga(nslot, 1)

    obuf[slot] = rmsnorm(x1[...], fg_ref, F32).reshape(TT, BATCH, D_MODEL)

    @pl.when(ti == n_steps - 1)
    def _():
        start_copies(o_copy, ti, slot)
        wait_slot(xbuf, sem_x, xs2)
        wait_slot(pbuf, sem_p, xs1)
        wait_slot(pbuf, sem_p, xs2)
        wait_slot(obuf, sem_o, nslot)
        wait_slot(obuf, sem_o, slot)


def _const_spec(shape):
    zeros = (0,) * len(shape)
    return pl.BlockSpec(shape, lambda i: zeros, pipeline_mode=pl.Buffered(1))


@jax.jit
def kernel(x, p, norm_g, w_in, conv_w, conv_b, lru_w_a, lru_b_a, lru_w_x, lru_b_x, lru_lambda,
           pool_w, pool_scale, w_proj_lru, w_proj_pool, w_out, ple_norm_g, w_ple_gate,
           w_ple_proj, final_g):
    assert x.shape == (BATCH, SEQ, D_MODEL) and p.shape == (1, BATCH, SEQ, P_DIM)
    row = lambda v: v.reshape(1, -1)
    operands = [
        x, p, row(norm_g[0]), w_in, conv_w[0], row(conv_b[0]), lru_w_a, row(lru_b_a[0]), lru_w_x,
        row(lru_b_x[0]), row(lru_lambda[0]), pool_w, row(pool_scale[0]), w_proj_lru, w_proj_pool,
        w_out, row(ple_norm_g[0]), w_ple_gate, w_ple_proj, row(final_g),
    ]
    in_specs = [pl.BlockSpec(memory_space=pl.ANY) if v.ndim > 2 else _const_spec(v.shape)
                for v in operands]
    scratch = [
        pltpu.VMEM((X_SLOTS, TT, BATCH, D_MODEL), F32),
        pltpu.VMEM((X_SLOTS, TT, BATCH, P_DIM), F32),
        pltpu.VMEM((2, TT, BATCH, D_MODEL), F32),
        pltpu.SemaphoreType.DMA((X_SLOTS,)),
        pltpu.SemaphoreType.DMA((X_SLOTS,)),
        pltpu.SemaphoreType.DMA((2,)),
        pltpu.SemaphoreType.DMA((N_TB,)),
        pltpu.VMEM((D_MODEL, IN_COLS), BF16),
        pltpu.VMEM((LRU_HEADS, LRU_HEAD_DIM, 2 * LRU_HEAD_DIM), BF16),
        pltpu.VMEM((POOL_GROUPS, POOL_GROUP_DIM, POOL_GROUP_DIM), BF16),
        pltpu.VMEM((LRU_WIDTH, D_MODEL), BF16),
        pltpu.VMEM((POOL_WIDTH, D_MODEL), BF16),
        pltpu.VMEM((D_MODEL, D_MODEL), BF16),
        pltpu.VMEM((D_MODEL, D_MODEL), BF16),
        pltpu.VMEM((P_DIM, D_MODEL), BF16),
        pltpu.VMEM((2, ROWS, D_MODEL), BF16),
        pltpu.VMEM((ROWS, D_MODEL), BF16),
        pltpu.VMEM((ROWS, P_DIM), BF16),
        pltpu.VMEM((N_CHUNKS, ROWS + CONV_TAIL, CW), F32),
        pltpu.VMEM((ROWS + POOL_TAIL, POOL_WIDTH), F32),
        pltpu.VMEM((BATCH, LRU_WIDTH), F32),
        pltpu.VMEM((N_TB, ROWS, CW), F32),
        pltpu.VMEM((2, ROWS, CW), BF16),
        pltpu.VMEM((ROWS, LRU_WIDTH), BF16),
        pltpu.VMEM((ROWS, POOL_WIDTH), BF16),
        pltpu.VMEM((POOL_GROUPS, ROWS, POOL_GROUP_DIM), BF16),
        pltpu.VMEM((ROWS, D_MODEL), BF16),
        pltpu.VMEM((ROWS, D_MODEL), F32),
    ]
    scratch_vmem_bytes = sum(math.prod(s.shape) * jnp.dtype(s.dtype).itemsize
                             for s in scratch if s.memory_space == pltpu.VMEM)
    return pl.pallas_call(
        _block_kernel,
        grid=(SEQ // TT,),
        in_specs=in_specs,
        out_specs=pl.BlockSpec(memory_space=pl.ANY),
        out_shape=jax.ShapeDtypeStruct((BATCH, SEQ, D_MODEL), F32),
        scratch_shapes=scratch,
        compiler_params=pltpu.CompilerParams(
            dimension_semantics=("arbitrary",),
            vmem_limit_bytes=scratch_vmem_bytes + COMPILER_VMEM_ALLOWANCE_BYTES,
        ),
        name="rglru_pool_block",
    )(*operands)
```

```python
import math

import jax
import jax.numpy as jnp
from jax import lax
from jax.experimental import pallas as pl
from jax.experimental.pallas import tpu as pltpu

D_MODEL = 1024
BATCH = 16
SEQ = 2048
P_DIM = 256
LRU_WIDTH = 1024
LRU_HEADS = 8
LRU_HEAD_DIM = 128
CONV_WIDTH = 4
LRU_C = 8.0
POOL_WIDTH = 512
POOL_WINDOWS = (2, 4, 8, 16)
POOL_GROUPS = 4
POOL_GROUP_DIM = 128
MAX_WIN = 16
EPS = 1e-6

OFF_XA = 0
OFF_GA = LRU_WIDTH
OFF_XB = 2 * LRU_WIDTH
OFF_GB = OFF_XB + POOL_WIDTH
OFF_MA = OFF_GB + POOL_WIDTH
OFF_MB = OFF_MA + D_MODEL
IN_COLS = OFF_MB + D_MODEL

TT = 32
ROWS = TT * BATCH
CW = 256
N_CHUNKS = D_MODEL // CW
HEADS_PER_CHUNK = CW // LRU_HEAD_DIM
CONV_TAIL = (CONV_WIDTH - 1) * BATCH
POOL_TAIL = MAX_WIN * BATCH
X_SLOTS = 3
COMPILER_VMEM_ALLOWANCE_BYTES = 4 * 1024 * 1024

TB_GA = 0
TB_G0, TB_G1, TB_A, TB_U = 6, 8, 10, 12
TB_MA, TB_MB = 14, 18
TB_GB = 22
N_TB = 24
TB_PE = 6

F32 = jnp.float32
BF16 = jnp.bfloat16
TINY = 1e-30


def _tanh_p1(vh):
    return jnp.tanh(vh) + 1.0


def _dot(a, b):
    return jnp.dot(a, b, preferred_element_type=F32)


def _block_kernel(x_hbm, p_hbm, ng_ref, win_hbm, cw_ref, cb_ref, wa_hbm, ba_ref, wx_hbm, bx_ref,
                  lam_ref, pw_hbm, ps_ref, plru_hbm, ppool_hbm, wout_hbm, pg_ref, wpg_hbm,
                  wpe_hbm, fg_ref,
                  o_hbm,
                  xbuf, pbuf, obuf, sem_x, sem_p, sem_o, sem_w,
                  win_ref, wax_ref, pw_ref, plru_ref, ppool_ref, wout_ref, wpg_ref, wpe_ref,
                  hb_in, hb2, pbf, xa_ext, xb_ext, hst, tb, xcb, ya, yb, dbuf, mbuf, x1):
    ti = pl.program_id(0)
    n_steps = pl.num_programs(0)
    slot = ti % 2
    nxt = jnp.minimum(ti + 1, n_steps - 1)
    nxt2 = jnp.minimum(ti + 2, n_steps - 1)
    nslot = 1 - slot
    xs, xs1, xs2 = ti % X_SLOTS, (ti + 1) % X_SLOTS, (ti + 2) % X_SLOTS

    def x_copy(step, sl, b):
        return pltpu.make_async_copy(x_hbm.at[b, pl.ds(step * TT, TT), :], xbuf.at[sl, :, b, :],
                                     sem_x.at[sl])

    def p_copy(step, sl, b):
        return pltpu.make_async_copy(p_hbm.at[0, b, pl.ds(step * TT, TT), :], pbuf.at[sl, :, b, :],
                                     sem_p.at[sl])

    def o_copy(step, sl, b):
        return pltpu.make_async_copy(obuf.at[sl, :, b, :], o_hbm.at[b, pl.ds(step * TT, TT), :],
                                     sem_o.at[sl])

    def start_copies(copy, step, sl):
        for b in range(BATCH):
            copy(step, sl, b).start()

    def wait_slot(buf, sem, sl):
        pltpu.make_async_copy(buf.at[sl], buf.at[sl], sem.at[sl]).wait()

    def x_rows(sl, r0, nrows, c0=0, ncols=D_MODEL):
        t0, nt = r0 // BATCH, nrows // BATCH
        return xbuf[sl, pl.ds(t0, nt), :, c0:c0 + ncols].reshape(nrows, ncols)

    def rmsnorm(xv, g_ref, cast):
        ms = jnp.mean(xv * xv, axis=-1, keepdims=True)
        return ((xv * lax.rsqrt(ms + EPS)) * g_ref[...]).astype(cast)

    def win(off, c):
        return win_ref[:, off + c * CW:off + (c + 1) * CW]

    def norm_in(xsl, sl):
        hb_in[sl] = rmsnorm(x_rows(xsl, 0, ROWS), ng_ref, BF16)

    def dots_xa_ga(sl, c):
        xa_ext[c, CONV_TAIL:CONV_TAIL + ROWS, :] = _dot(hb_in[sl], win(OFF_XA, c))
        tb[TB_GA + c] = _dot(hb_in[sl], win(OFF_GA, c))

    def weight_tiles():
        tiles = []

        def add(src, dst, nrows, ncols, scale):
            for r0 in range(0, nrows, ROWS):
                nr = min(ROWS, nrows - r0)
                for c0 in range(0, ncols, CW):
                    tiles.append((src.at[0, pl.ds(r0, nr), pl.ds(c0, CW)], nr, CW,
                                  dst.at[pl.ds(r0, nr), pl.ds(c0, CW)],
                                  scale(c0) if callable(scale) else scale))

        halved_in = lambda c0: 0.5 if (OFF_GA <= c0 < OFF_XB or c0 >= OFF_GB) else 1.0
        add(win_hbm, win_ref, D_MODEL, IN_COLS, halved_in)
        add(plru_hbm, plru_ref, LRU_WIDTH, D_MODEL, 0.5)
        add(ppool_hbm, ppool_ref, POOL_WIDTH, D_MODEL, 0.5)
        add(wout_hbm, wout_ref, D_MODEL, D_MODEL, 1.0)
        add(wpg_hbm, wpg_ref, D_MODEL, D_MODEL, 0.5)
        add(wpe_hbm, wpe_ref, P_DIM, D_MODEL, 0.5)
        for h in range(LRU_HEADS):
            for part, src in enumerate((wa_hbm, wx_hbm)):
                tiles.append((src.at[0, h], LRU_HEAD_DIM, LRU_HEAD_DIM,
                              wax_ref.at[h, :, pl.ds(part * LRU_HEAD_DIM, LRU_HEAD_DIM)], 0.5))
        for g in range(POOL_GROUPS):
            tiles.append((pw_hbm.at[0, g], POOL_GROUP_DIM, POOL_GROUP_DIM, pw_ref.at[g], 1.0))
        return tiles

    def load_weights():
        tiles = weight_tiles()
        copies = [pltpu.make_async_copy(src, tb.at[i % N_TB, pl.ds(0, nr), pl.ds(0, nc)], sem_w.at[i % N_TB])
                  for i, (src, nr, nc, _, _) in enumerate(tiles)]
        for i in range(len(tiles) + N_TB):
            done = i - N_TB
            if done >= 0:
                _, nr, nc, dst, scale = tiles[done]
                copies[done].wait()
                v = tb[done % N_TB, 0:nr, 0:nc]
                dst[...] = (v if scale == 1.0 else v * scale).astype(BF16)
            if i < len(tiles):
                copies[i].start()

    @pl.when(ti == 0)
    def _():
        start_copies(x_copy, 0, 0)
        start_copies(x_copy, 1, 1)
        start_copies(p_copy, 0, 0)
        start_copies(p_copy, 1, 1)
        load_weights()
        xa_ext[:, 0:CONV_TAIL, :] = jnp.zeros((N_CHUNKS, CONV_TAIL, CW), F32)
        xb_ext[0:POOL_TAIL, :] = jnp.zeros((POOL_TAIL, POOL_WIDTH), F32)
        hst[...] = jnp.zeros((BATCH, LRU_WIDTH), F32)
        wait_slot(xbuf, sem_x, 0)
        norm_in(0, 0)
        dots_xa_ga(0, 0)
        dots_xa_ga(0, 1)

    @pl.when(ti >= 1)
    def _():
        start_copies(o_copy, ti - 1, nslot)

    @pl.when(ti >= 2)
    def _():
        wait_slot(obuf, sem_o, slot)

    wait_slot(pbuf, sem_p, xs)
    wait_slot(xbuf, sem_x, xs1)

    hb = hb_in.at[slot]
    pbf[...] = pbuf[xs].reshape(ROWS, P_DIM).astype(BF16)

    lam = lam_ref[...]
    half_c = (-0.5 * LRU_C) * (jnp.maximum(-lam, 0.0) + jnp.log1p(jnp.exp(-jnp.abs(lam))))

    def conv(c):
        q, c0 = c % 2, c * CW
        acc = cb_ref[:, c0:c0 + CW]
        for k in range(CONV_WIDTH):
            acc = acc + xa_ext[c, k * BATCH:k * BATCH + ROWS, :] * cw_ref[k:k + 1, c0:c0 + CW]
        xcb[q] = acc.astype(BF16)
        xa_ext[c, 0:CONV_TAIL, :] = xa_ext[c, ROWS:ROWS + CONV_TAIL, :]
        return acc

    def dots_gates(c):
        q = c % 2
        for hh, tg in enumerate((TB_G0, TB_G1)):
            l0 = hh * LRU_HEAD_DIM
            tb[tg + q] = _dot(xcb[q, :, l0:l0 + LRU_HEAD_DIM], wax_ref[c * HEADS_PER_CHUNK + hh])

    def gates(c, xc):
        q = c % 2
        for hh, tg in enumerate((TB_G0, TB_G1)):
            l0 = hh * LRU_HEAD_DIM
            ch = slice(c * CW + l0, c * CW + l0 + LRU_HEAD_DIM)
            lanes = slice(l0, l0 + LRU_HEAD_DIM)
            r_tanh = jnp.tanh(tb[tg + q, :, 0:LRU_HEAD_DIM] + 0.5 * ba_ref[:, ch])
            i_gate = 0.5 * jnp.tanh(tb[tg + q, :, LRU_HEAD_DIM:2 * LRU_HEAD_DIM] + 0.5 * bx_ref[:, ch]) + 0.5
            a = jnp.exp(half_c[:, ch] * (r_tanh + 1.0))
            y = 1.0 - a * a
            mult = y * lax.rsqrt(jnp.maximum(y, TINY))
            tb[TB_A + q, :, lanes] = a
            tb[TB_U + q, :, lanes] = mult * (i_gate * xc[:, lanes])

    def scan(c):
        q, c0 = c % 2, c * CW
        ga = tb[TB_GA + c]
        tb[TB_GA + c] = _tanh_p1(ga) * ga
        h = hst[:, c0:c0 + CW]
        for t in range(TT):
            rows = pl.ds(t * BATCH, BATCH)
            h = tb[TB_A + q, rows, :] * h + tb[TB_U + q, rows, :]
            ya[rows, c0:c0 + CW] = (h * tb[TB_GA + c, rows, :]).astype(BF16)
        hst[:, c0:c0 + CW] = h

    def dots_pool_in():
        xb_ext[POOL_TAIL:POOL_TAIL + ROWS, :] = _dot(hb[...], win_ref[:, OFF_XB:OFF_XB + POOL_WIDTH])
        for half in range(POOL_WIDTH // CW):
            tb[TB_GB + half] = _dot(hb[...], win(OFF_GB, half))

    def pool_windows(g):
        k = POOL_WINDOWS[g]
        lanes = slice(g * POOL_GROUP_DIM, (g + 1) * POOL_GROUP_DIM)
        cur = xb_ext[POOL_TAIL:POOL_TAIL + ROWS, lanes]
        s = cur
        for j in range(1, k):
            s = s + xb_ext[POOL_TAIL - j * BATCH:POOL_TAIL - j * BATCH + ROWS, lanes]
        row = lax.broadcasted_iota(jnp.int32, (ROWS, POOL_GROUP_DIM), 0)
        pos = ti * TT + row // BATCH
        cnt = jnp.minimum(pos + 1, k).astype(F32)
        dbuf[g] = (s / cnt - cur).astype(BF16)

    def pool_out(g):
        lanes = slice(g * POOL_GROUP_DIM, (g + 1) * POOL_GROUP_DIM)
        half, gg = divmod(g, CW // POOL_GROUP_DIM)
        gb = tb[TB_GB + half, :, gg * POOL_GROUP_DIM:(gg + 1) * POOL_GROUP_DIM]
        y = _dot(dbuf[g], pw_ref[g]) * ps_ref[:, lanes]
        yb[:, lanes] = (y * (_tanh_p1(gb) * gb)).astype(BF16)

    def dots_merge_logits(c):
        tb[TB_MA + c] = _dot(hb[...], win(OFF_MA, c))
        tb[TB_MB + c] = _dot(hb[...], win(OFF_MB, c))

    for c in range(N_CHUNKS):
        xc = conv(c)
        dots_gates(c)
        if c + 2 < N_CHUNKS:
            dots_xa_ga(slot, c + 2)
        if c == 0:
            dots_pool_in()
        else:
            dots_merge_logits(c - 1)
        gates(c, xc)
        scan(c)
        pool_windows(c)
        pool_out(c)
    dots_merge_logits(N_CHUNKS - 1)
    xb_ext[0:POOL_TAIL, :] = xb_ext[ROWS:ROWS + POOL_TAIL, :]

    for c in range(N_CHUNKS):
        c0 = c * CW
        pa = _dot(ya[...], plru_ref[:, c0:c0 + CW])
        pb = _dot(yb[...], ppool_ref[:, c0:c0 + CW])
        mbuf[:, c0:c0 + CW] = (_tanh_p1(tb[TB_MA + c]) * pa + _tanh_p1(tb[TB_MB + c]) * pb).astype(BF16)

    for c in range(N_CHUNKS):
        c0 = c * CW
        x1[:, c0:c0 + CW] = x_rows(xs, 0, ROWS, c0, CW) + _dot(mbuf[...], wout_ref[:, c0:c0 + CW])
    norm_in(xs1, nslot)
    start_copies(x_copy, nxt2, xs2)
    start_copies(p_copy, nxt2, xs2)

    for c in range(N_CHUNKS):
        tb[TB_PE + c] = _dot(pbf[...], wpe_ref[:, c * CW:(c + 1) * CW])
    hb2[...] = rmsnorm(x1[...], pg_ref, BF16)

    for c in range(N_CHUNKS):
        c0 = c * CW
        gate = _dot(hb2[...], wpg_ref[:, c0:c0 + CW])
        x1[:, c0:c0 + CW] = x1[:, c0:c0 + CW] + _tanh_p1(gate) * tb[TB_PE + c]
    dots_xa_ga(nslot, 0)
    dots_xa_ga(nslot, 1)

    obuf[slot] = rmsnorm(x1[...], fg_ref, F32).reshape(TT, BATCH, D_MODEL)

    @pl.when(ti == n_steps - 1)
    def _():
        start_copies(o_copy, ti, slot)
        wait_slot(xbuf, sem_x, xs2)
        wait_slot(pbuf, sem_p, xs1)
        wait_slot(pbuf, sem_p, xs2)
        wait_slot(obuf, sem_o, nslot)
        wait_slot(obuf, sem_o, slot)


def _const_spec(shape):
    zeros = (0,) * len(shape)
    return pl.BlockSpec(shape, lambda i: zeros, pipeline_mode=pl.Buffered(1))


@jax.jit
def kernel(x, p, norm_g, w_in, conv_w, conv_b, lru_w_a, lru_b_a, lru_w_x, lru_b_x, lru_lambda,
           pool_w, pool_scale, w_proj_lru, w_proj_pool, w_out, ple_norm_g, w_ple_gate,
           w_ple_proj, final_g):
    assert x.shape == (BATCH, SEQ, D_MODEL) and p.shape == (1, BATCH, SEQ, P_DIM)
    row = lambda v: v.reshape(1, -1)
    operands = [
        x, p, row(norm_g[0]), w_in, conv_w[0], row(conv_b[0]), lru_w_a, row(lru_b_a[0]), lru_w_x,
        row(lru_b_x[0]), row(lru_lambda[0]), pool_w, row(pool_scale[0]), w_proj_lru, w_proj_pool,
        w_out, row(ple_norm_g[0]), w_ple_gate, w_ple_proj, row(final_g),
    ]
    in_specs = [pl.BlockSpec(memory_space=pl.ANY) if v.ndim > 2 else _const_spec(v.shape)
                for v in operands]
    scratch = [
        pltpu.VMEM((X_SLOTS, TT, BATCH, D_MODEL), F32),
        pltpu.VMEM((X_SLOTS, TT, BATCH, P_DIM), F32),
        pltpu.VMEM((2, TT, BATCH, D_MODEL), F32),
        pltpu.SemaphoreType.DMA((X_SLOTS,)),
        pltpu.SemaphoreType.DMA((X_SLOTS,)),
        pltpu.SemaphoreType.DMA((2,)),
        pltpu.SemaphoreType.DMA((N_TB,)),
        pltpu.VMEM((D_MODEL, IN_COLS), BF16),
        pltpu.VMEM((LRU_HEADS, LRU_HEAD_DIM, 2 * LRU_HEAD_DIM), BF16),
        pltpu.VMEM((POOL_GROUPS, POOL_GROUP_DIM, POOL_GROUP_DIM), BF16),
        pltpu.VMEM((LRU_WIDTH, D_MODEL), BF16),
        pltpu.VMEM((POOL_WIDTH, D_MODEL), BF16),
        pltpu.VMEM((D_MODEL, D_MODEL), BF16),
        pltpu.VMEM((D_MODEL, D_MODEL), BF16),
        pltpu.VMEM((P_DIM, D_MODEL), BF16),
        pltpu.VMEM((2, ROWS, D_MODEL), BF16),
        pltpu.VMEM((ROWS, D_MODEL), BF16),
        pltpu.VMEM((ROWS, P_DIM), BF16),
        pltpu.VMEM((N_CHUNKS, ROWS + CONV_TAIL, CW), F32),
        pltpu.VMEM((ROWS + POOL_TAIL, POOL_WIDTH), F32),
        pltpu.VMEM((BATCH, LRU_WIDTH), F32),
        pltpu.VMEM((N_TB, ROWS, CW), F32),
        pltpu.VMEM((2, ROWS, CW), BF16),
        pltpu.VMEM((ROWS, LRU_WIDTH), BF16),
        pltpu.VMEM((ROWS, POOL_WIDTH), BF16),
        pltpu.VMEM((POOL_GROUPS, ROWS, POOL_GROUP_DIM), BF16),
        pltpu.VMEM((ROWS, D_MODEL), BF16),
        pltpu.VMEM((ROWS, D_MODEL), F32),
    ]
    scratch_vmem_bytes = sum(math.prod(s.shape) * jnp.dtype(s.dtype).itemsize
                             for s in scratch if s.memory_space == pltpu.VMEM)
    return pl.pallas_call(
        _block_kernel,
        grid=(SEQ // TT,),
        in_specs=in_specs,
        out_specs=pl.BlockSpec(memory_space=pl.ANY),
        out_shape=jax.ShapeDtypeStruct((BATCH, SEQ, D_MODEL), F32),
        scratch_shapes=scratch,
        compiler_params=pltpu.CompilerParams(
            dimension_semantics=("arbitrary",),
            vmem_limit_bytes=scratch_vmem_bytes + COMPILER_VMEM_ALLOWANCE_BYTES,
        ),
        name="rglru_pool_block",
    )(*operands)
```
